```python
import math
import jax, jax.numpy as jnp
from jax import lax
import numpy as np

D_MODEL = 4096
BATCH = 8
SEQ = 2048
DEPTH = 4

CHUNK = 64
Q_BLOCK = 128
N_A_LAYERS = DEPTH // 2
N_B_LAYERS = DEPTH - N_A_LAYERS
SB_HEAD_DIM = 128
SB_HEADS = D_MODEL // SB_HEAD_DIM
DIFF_HEAD_DIM = 128
DIFF_HEADS = D_MODEL // (2 * DIFF_HEAD_DIM)
D_FF = 3 * D_MODEL // 2
RMS_EPS = 1e-6
LAMBDA_STD = 0.1
NORM_NOISE = 0.01

kernel_name = "yoco_stickbreak_diffattn_macaron"


def rms_norm(x, g):
    xf = x.astype(jnp.float32)
    y = xf * lax.rsqrt(jnp.mean(xf * xf, axis=-1, keepdims=True) + RMS_EPS)
    return y.astype(x.dtype) * g


def swiglu(x, w_in, w_out):
    gate, up = jnp.split(x @ w_in, 2, axis=-1)
    return (jax.nn.silu(gate) * up) @ w_out


def macaron_half(h, g, w_in, w_out):
    return h + 0.5 * swiglu(rms_norm(h, g), w_in, w_out)


def stick_breaking_attention(q, k, v):
    seq = q.shape[1]
    scale = 1.0 / math.sqrt(SB_HEAD_DIM)
    outs = []
    for q0 in range(0, seq, Q_BLOCK):
        lk = q0 + Q_BLOCK
        z = jnp.einsum('bqhd,bkhd->bhqk', q[:, q0:lk], k[:, :lk]).astype(jnp.float32) * scale
        t = q0 + jnp.arange(Q_BLOCK)[:, None]
        s = jnp.arange(lk)[None, :]
        strict = s < t
        log_beta = jax.nn.log_sigmoid(z)
        log_1m = jnp.where(strict, jax.nn.log_sigmoid(-z), 0.0)
        later = lax.cumsum(log_1m, axis=3, reverse=True) - log_1m
        a = jnp.where(strict, jnp.exp(log_beta + later), 0.0)
        outs.append(jnp.einsum('bhqk,bkhd->bqhd', a.astype(v.dtype), v[:, :lk]))
    return jnp.concatenate(outs, axis=1)


def alibi_slopes(n_heads):
    return 2.0 ** (-8.0 * jnp.arange(1, n_heads + 1, dtype=jnp.float32) / n_heads)


def differential_attention(q, k, v, lam):
    seq = q.shape[1]
    scale = 1.0 / math.sqrt(DIFF_HEAD_DIM)
    slopes = alibi_slopes(DIFF_HEADS)[:, None, None]
    outs = []
    for q0 in range(0, seq, Q_BLOCK):
        lk = q0 + Q_BLOCK
        scores = jnp.einsum('bqhcd,bkhcd->bchqk', q[:, q0:lk], k[:, :lk]).astype(jnp.float32) * scale
        t = q0 + jnp.arange(Q_BLOCK)[:, None]
        s = jnp.arange(lk)[None, :]
        bias = -slopes * jnp.abs(t - s).astype(jnp.float32)
        allowed = (s // CHUNK) <= (t // CHUNK)
        p = jax.nn.softmax(jnp.where(allowed, scores + bias, -jnp.inf), axis=-1)
        attn = p[:, 0] - lam * p[:, 1]
        outs.append(jnp.einsum('bhqk,bkhe->bqhe', attn.astype(v.dtype), v[:, :lk]))
    return jnp.concatenate(outs, axis=1)


def setup_inputs(seed: int = 0) -> dict:
    key = jax.random.key(seed)
    ks = jax.random.split(key, 17)
    f32 = jnp.float32
    D, F = D_MODEL, D_FF

    def w(k, shape, fan_in):
        return jax.random.normal(k, shape, f32) * (fan_in ** -0.5)

    def gain(k, shape):
        return 1.0 + NORM_NOISE * jax.random.normal(k, shape, f32)

    return {
        'x': jax.random.normal(ks[0], (BATCH, SEQ, D), f32),
        'ffn_norm': gain(ks[1], (DEPTH, 2, D)),
        'w_ffn_in': w(ks[2], (DEPTH, 2, D, 2 * F), D),
        'w_ffn_out': w(ks[3], (DEPTH, 2, F, D), F),
        'attn_norm': gain(ks[4], (DEPTH, D)),
        'w_qkv_a': w(ks[5], (N_A_LAYERS, D, 3 * D), D),
        'w_o_a': w(ks[6], (N_A_LAYERS, D, D), D),
        'kv_norm': gain(ks[7], (D,)),
        'w_kv_b': w(ks[8], (D, 2 * D), D),
        'w_q_b': w(ks[9], (N_B_LAYERS, D, D), D),
        'lambda_q1': LAMBDA_STD * jax.random.normal(ks[10], (N_B_LAYERS, DIFF_HEAD_DIM), f32),
        'lambda_k1': LAMBDA_STD * jax.random.normal(ks[11], (N_B_LAYERS, DIFF_HEAD_DIM), f32),
        'lambda_q2': LAMBDA_STD * jax.random.normal(ks[12], (N_B_LAYERS, DIFF_HEAD_DIM), f32),
        'lambda_k2': LAMBDA_STD * jax.random.normal(ks[13], (N_B_LAYERS, DIFF_HEAD_DIM), f32),
        'subln_norm': gain(ks[14], (N_B_LAYERS, 2 * DIFF_HEAD_DIM)),
        'w_o_b': w(ks[15], (N_B_LAYERS, D, D), D),
        'final_norm': gain(ks[16], (D,)),
    }


def reference(x, ffn_norm, w_ffn_in, w_ffn_out, attn_norm, w_qkv_a, w_o_a, kv_norm,
              w_kv_b, w_q_b, lambda_q1, lambda_k1, lambda_q2, lambda_k2, subln_norm,
              w_o_b, final_norm):
    B, S, D = x.shape
    f32 = jnp.float32
    h = x
    kv_shared = None
    for l in range(DEPTH):
        h = macaron_half(h, ffn_norm[l, 0], w_ffn_in[l, 0], w_ffn_out[l, 0])
        xn = rms_norm(h, attn_norm[l])
        if l < N_A_LAYERS:
            q, k, v = jnp.split(xn @ w_qkv_a[l], 3, axis=-1)
            shp = (B, S, SB_HEADS, SB_HEAD_DIM)
            o = stick_breaking_attention(q.reshape(shp), k.reshape(shp), v.reshape(shp))
            h = h + o.reshape(B, S, D) @ w_o_a[l]
        else:
            j = l - N_A_LAYERS
            lam_init = 0.8 - 0.6 * math.exp(-0.3 * l)
            q = (xn @ w_q_b[j]).reshape(B, S, DIFF_HEADS, 2, DIFF_HEAD_DIM)
            lam = (jnp.exp(jnp.sum(lambda_q1[j].astype(f32) * lambda_k1[j].astype(f32)))
                   - jnp.exp(jnp.sum(lambda_q2[j].astype(f32) * lambda_k2[j].astype(f32)))
                   + lam_init)
            k_sh, v_sh = kv_shared
            o = differential_attention(q, k_sh, v_sh, lam)
            o = rms_norm(o, subln_norm[j]) * (1.0 - lam_init)
            h = h + o.reshape(B, S, D) @ w_o_b[j]
        h = macaron_half(h, ffn_norm[l, 1], w_ffn_in[l, 1], w_ffn_out[l, 1])
        if l == N_A_LAYERS - 1:
            k_sh, v_sh = jnp.split(rms_norm(h, kv_norm) @ w_kv_b, 2, axis=-1)
            kv_shared = (k_sh.reshape(B, S, DIFF_HEADS, 2, DIFF_HEAD_DIM),
                         v_sh.reshape(B, S, DIFF_HEADS, 2 * DIFF_HEAD_DIM))
    return rms_norm(h, final_norm)
```

```python
import functools
import math

import jax
import jax.numpy as jnp
from jax import lax
from jax.experimental import pallas as pl
from jax.experimental.pallas import tpu as pltpu

F32 = jnp.float32
BF16 = jnp.bfloat16

HEAD_DIM = 128
STREAM_CHUNK = 64
RMS_EPS = 1e-6
ATTN_BLOCK = 256
V7X_VMEM_BYTES = 64 * 1024 * 1024
VMEM_LIMIT_BYTES = 56 * 1024 * 1024


def _compiler_params(semantics):
    return pltpu.CompilerParams(dimension_semantics=semantics,
                                vmem_limit_bytes=VMEM_LIMIT_BYTES)


def _pick(n, pref):
    if n <= pref:
        return n
    t = pref
    while n % t:
        t //= 2
    return t


def _rmsnorm_kernel(x_ref, g_ref, o_ref):
    x = x_ref[...]
    ms = jnp.mean(x * x, axis=-1, keepdims=True)
    o_ref[...] = (x * lax.rsqrt(ms + RMS_EPS) * g_ref[...]).astype(o_ref.dtype)


def rmsnorm(x, g, out_dtype):
    m, d = x.shape
    bm = _pick(m, 512)
    return pl.pallas_call(
        _rmsnorm_kernel,
        grid=(m // bm,),
        in_specs=[pl.BlockSpec((bm, d), lambda i: (i, 0)),
                  pl.BlockSpec((1, d), lambda i: (0, 0))],
        out_specs=pl.BlockSpec((bm, d), lambda i: (i, 0)),
        out_shape=jax.ShapeDtypeStruct((m, d), out_dtype),
        compiler_params=_compiler_params(("arbitrary",)),
        name="rmsnorm",
    )(x, g.reshape(1, d))


def _mm_kernel(x_ref, w_ref, o_ref):
    o_ref[...] = jnp.dot(x_ref[...], w_ref[...],
                         preferred_element_type=F32).astype(o_ref.dtype)


def _mm_residual_kernel(x_ref, w_ref, h_ref, o_ref, *, alpha):
    acc = jnp.dot(x_ref[...], w_ref[...], preferred_element_type=F32)
    o_ref[...] = h_ref[...] + alpha * acc


def _mm_swiglu_kernel(x_ref, wg_ref, wu_ref, o_ref):
    x = x_ref[...]
    gate = jnp.dot(x, wg_ref[...], preferred_element_type=F32)
    up = jnp.dot(x, wu_ref[...], preferred_element_type=F32)
    o_ref[...] = (gate * jax.nn.sigmoid(gate) * up).astype(o_ref.dtype)


def matmul(x, w, *, bm=1024, bn=1024):
    m, k = x.shape
    n = w.shape[1]
    bm, bn = _pick(m, bm), _pick(n, bn)
    return pl.pallas_call(
        _mm_kernel,
        grid=(m // bm, n // bn),
        in_specs=[pl.BlockSpec((bm, k), lambda i, j: (i, 0)),
                  pl.BlockSpec((k, bn), lambda i, j: (0, j))],
        out_specs=pl.BlockSpec((bm, bn), lambda i, j: (i, j)),
        out_shape=jax.ShapeDtypeStruct((m, n), BF16),
        compiler_params=_compiler_params(("arbitrary", "arbitrary")),
        name="matmul",
    )(x, w)


def matmul_residual(x, w, h, alpha, *, bm=1024, bn=512):
    m, k = x.shape
    n = w.shape[1]
    bm, bn = _pick(m, bm), _pick(n, bn)
    return pl.pallas_call(
        functools.partial(_mm_residual_kernel, alpha=alpha),
        grid=(m // bm, n // bn),
        in_specs=[pl.BlockSpec((bm, k), lambda i, j: (i, 0)),
                  pl.BlockSpec((k, bn), lambda i, j: (0, j)),
                  pl.BlockSpec((bm, bn), lambda i, j: (i, j))],
        out_specs=pl.BlockSpec((bm, bn), lambda i, j: (i, j)),
        out_shape=jax.ShapeDtypeStruct((m, n), F32),
        compiler_params=_compiler_params(("arbitrary", "arbitrary")),
        name="matmul_residual",
    )(x, w, h)


def matmul_swiglu(x, w_in, *, bm=1024, bn=512):
    m, k = x.shape
    f = w_in.shape[1] // 2
    bm, bn = _pick(m, bm), _pick(f, bn)
    nf = f // bn
    return pl.pallas_call(
        _mm_swiglu_kernel,
        grid=(m // bm, nf),
        in_specs=[pl.BlockSpec((bm, k), lambda i, j: (i, 0)),
                  pl.BlockSpec((k, bn), lambda i, j: (0, j)),
                  pl.BlockSpec((k, bn), lambda i, j: (0, nf + j))],
        out_specs=pl.BlockSpec((bm, bn), lambda i, j: (i, j)),
        out_shape=jax.ShapeDtypeStruct((m, f), BF16),
        compiler_params=_compiler_params(("arbitrary", "arbitrary")),
        name="matmul_swiglu",
    )(x, w_in, w_in)


def _sb_attention_kernel(q_ref, k_ref, v_ref, o_ref, qt_ref, vt_ref, *, seq, blk, scale):
    nblk = seq // blk
    for b in range(nblk):
        rows = slice(b * blk, (b + 1) * blk)
        qt_ref[b] = q_ref[rows, :].astype(F32).T.astype(BF16)
        vt_ref[b] = v_ref[rows, :].astype(F32).T.astype(BF16)

    row = lax.broadcasted_iota(jnp.int32, (blk, blk), 0)
    col = lax.broadcasted_iota(jnp.int32, (blk, blk), 1)
    upper = (col > row).astype(BF16)
    strict = row < col

    def block(i, j, carry, acc, masked):
        k_blk = k_ref[pl.ds(pl.multiple_of(j * blk, blk), blk), :]
        z = jnp.dot(k_blk, qt_ref[i], preferred_element_type=F32) * scale
        log1p_e = jnp.log(1.0 + jnp.exp(-jnp.abs(z)))
        log_beta = jnp.minimum(z, 0.0) - log1p_e
        log_1m = log_beta - z
        if masked:
            log_1m = jnp.where(strict, log_1m, 0.0)
        hi = log_1m.astype(BF16)
        lo = (log_1m - hi.astype(F32)).astype(BF16)
        later = (jnp.dot(upper, hi, preferred_element_type=F32)
                 + jnp.dot(upper, lo, preferred_element_type=F32) + carry)
        a = jnp.exp(log_beta + later)
        if masked:
            a = jnp.where(strict, a, 0.0)
        acc = acc + jnp.dot(vt_ref[j], a.astype(BF16), preferred_element_type=F32)
        carry = carry + jnp.sum(log_1m, axis=0, keepdims=True)
        return carry, acc

    def q_block(i, _):
        carry = jnp.zeros((1, blk), F32)
        acc = jnp.zeros((HEAD_DIM, blk), F32)
        carry, acc = block(i, i, carry, acc, True)

        def body(n, state):
            return block(i, i - 1 - n, state[0], state[1], False)

        carry, acc = lax.fori_loop(0, i, body, (carry, acc))
        o_ref[pl.ds(pl.multiple_of(i * blk, blk), blk), :] = acc.T.astype(o_ref.dtype)
        return 0

    lax.fori_loop(0, nblk, q_block, 0)


def sb_attention(qkv, batch, seq):
    d = qkv.shape[1] // 3
    heads = d // HEAD_DIM
    blk = _pick(seq, ATTN_BLOCK)
    kernel = functools.partial(_sb_attention_kernel, seq=seq, blk=blk,
                               scale=1.0 / math.sqrt(HEAD_DIM))
    return pl.pallas_call(
        kernel,
        grid=(batch, heads),
        in_specs=[pl.BlockSpec((seq, HEAD_DIM), lambda b, h: (b, h)),
                  pl.BlockSpec((seq, HEAD_DIM), lambda b, h: (b, heads + h)),
                  pl.BlockSpec((seq, HEAD_DIM), lambda b, h: (b, 2 * heads + h))],
        out_specs=pl.BlockSpec((seq, HEAD_DIM), lambda b, h: (b, h)),
        out_shape=jax.ShapeDtypeStruct((batch * seq, d), BF16),
        scratch_shapes=[pltpu.VMEM((seq // blk, HEAD_DIM, blk), BF16),
                        pltpu.VMEM((seq // blk, HEAD_DIM, blk), BF16)],
        compiler_params=_compiler_params(("arbitrary", "arbitrary")),
        name="sb_attention",
    )(qkv, qkv, qkv)


def _diff_attention_kernel(lq1_ref, lk1_ref, lq2_ref, lk2_ref, g_ref, q_ref, k_ref, v_ref,
                           o_ref, qt1_ref, qt2_ref, vt_ref, acc1_ref, acc2_ref,
                           *, seq, blk, heads, scale, lam_init):
    nblk = seq // blk
    dv = 2 * HEAD_DIM
    for b in range(nblk):
        rows = slice(b * blk, (b + 1) * blk)
        qt1_ref[b] = q_ref[rows, :HEAD_DIM].astype(F32).T.astype(BF16)
        qt2_ref[b] = q_ref[rows, HEAD_DIM:].astype(F32).T.astype(BF16)
        vt_ref[b] = v_ref[rows, :].astype(F32).T.astype(BF16)

    s1 = jnp.sum(lq1_ref[...] * lk1_ref[...], axis=-1, keepdims=True)
    s2 = jnp.sum(lq2_ref[...] * lk2_ref[...], axis=-1, keepdims=True)
    lam = jnp.exp(s1) - jnp.exp(s2) + lam_init

    head = jnp.full((1, 1), pl.program_id(1), jnp.int32).astype(F32)
    slope = jnp.exp2(-8.0 * (head + 1.0) / heads)

    row = lax.broadcasted_iota(jnp.int32, (blk, blk), 0)
    col = lax.broadcasted_iota(jnp.int32, (blk, blk), 1)
    dist = (col - row).astype(F32)
    allowed = (row // STREAM_CHUNK) <= (col // STREAM_CHUNK)

    def scores(k_half, qt, offset, masked):
        z = jnp.dot(k_half, qt, preferred_element_type=F32) * scale
        z = z - slope * jnp.abs(dist + offset)
        if masked:
            z = jnp.where(allowed, z, -jnp.inf)
        return z

    def update(z, m, l, acc_ref, vt):
        m_new = jnp.maximum(m, jnp.max(z, axis=0, keepdims=True))
        alpha = jnp.exp(m - m_new)
        p = jnp.exp(z - m_new)
        l = l * alpha + jnp.sum(p, axis=0, keepdims=True)
        acc_ref[...] = acc_ref[...] * alpha + jnp.dot(vt, p.astype(BF16),
                                                      preferred_element_type=F32)
        return m_new, l

    def block(i, j, state, masked):
        m1, l1, m2, l2 = state
        k_blk = k_ref[pl.ds(pl.multiple_of(j * blk, blk), blk), :]
        offset = ((i - j) * blk).astype(F32)
        vt = vt_ref[j]
        z1 = scores(k_blk[:, :HEAD_DIM], qt1_ref[i], offset, masked)
        m1, l1 = update(z1, m1, l1, acc1_ref, vt)
        z2 = scores(k_blk[:, HEAD_DIM:], qt2_ref[i], offset, masked)
        m2, l2 = update(z2, m2, l2, acc2_ref, vt)
        return m1, l1, m2, l2

    def q_block(i, _):
        acc1_ref[...] = jnp.zeros((dv, blk), F32)
        acc2_ref[...] = jnp.zeros((dv, blk), F32)
        neg = jnp.full((1, blk), -jnp.inf, F32)
        zero = jnp.zeros((1, blk), F32)
        state = block(i, i, (neg, zero, neg, zero), True)

        def body(n, st):
            return block(i, i - 1 - n, st, False)

        m1, l1, m2, l2 = lax.fori_loop(0, i, body, state)
        ot = acc1_ref[...] / l1 - lam * (acc2_ref[...] / l2)
        o = ot.T
        ms = jnp.mean(o * o, axis=-1, keepdims=True)
        y = o * lax.rsqrt(ms + RMS_EPS) * g_ref[...] * (1.0 - lam_init)
        o_ref[pl.ds(pl.multiple_of(i * blk, blk), blk), :] = y.astype(o_ref.dtype)
        return 0

    lax.fori_loop(0, nblk, q_block, 0)


def diff_attention(q, kv, lq1, lk1, lq2, lk2, gain, batch, seq, lam_init):
    d = q.shape[1]
    dv = 2 * HEAD_DIM
    heads = d // dv
    blk = _pick(seq, ATTN_BLOCK)
    kernel = functools.partial(_diff_attention_kernel, seq=seq, blk=blk, heads=heads,
                               scale=1.0 / math.sqrt(HEAD_DIM), lam_init=lam_init)
    vec = pl.BlockSpec((1, HEAD_DIM), lambda b, h: (0, 0))
    head_blk = pl.BlockSpec((seq, dv), lambda b, h: (b, h))
    return pl.pallas_call(
        kernel,
        grid=(batch, heads),
        in_specs=[vec, vec, vec, vec,
                  pl.BlockSpec((1, dv), lambda b, h: (0, 0)),
                  head_blk, head_blk,
                  pl.BlockSpec((seq, dv), lambda b, h: (b, heads + h))],
        out_specs=head_blk,
        out_shape=jax.ShapeDtypeStruct((batch * seq, d), BF16),
        scratch_shapes=[pltpu.VMEM((seq // blk, HEAD_DIM, blk), BF16),
                        pltpu.VMEM((seq // blk, HEAD_DIM, blk), BF16),
                        pltpu.VMEM((seq // blk, dv, blk), BF16),
                        pltpu.VMEM((dv, blk), F32),
                        pltpu.VMEM((dv, blk), F32)],
        compiler_params=_compiler_params(("arbitrary", "arbitrary")),
        name="diff_attention",
    )(lq1.reshape(1, -1), lk1.reshape(1, -1), lq2.reshape(1, -1), lk2.reshape(1, -1),
      gain.reshape(1, -1), q, kv, kv)


def _macaron_half(h, g, w_in, w_out):
    xn = rmsnorm(h, g, BF16)
    hid = matmul_swiglu(xn, w_in)
    return matmul_residual(hid, w_out, h, 0.5)


def kernel(x, ffn_norm, w_ffn_in, w_ffn_out, attn_norm, w_qkv_a, w_o_a, kv_norm, w_kv_b, w_q_b,
           lambda_q1, lambda_k1, lambda_q2, lambda_k2, subln_norm, w_o_b, final_norm):
    batch, seq, d = x.shape
    depth = ffn_norm.shape[0]
    n_a = w_qkv_a.shape[0]
    h = x.reshape(batch * seq, d)
    kv_shared = None
    for l in range(depth):
        h = _macaron_half(h, ffn_norm[l, 0], w_ffn_in[l, 0].astype(BF16),
                          w_ffn_out[l, 0].astype(BF16))
        xn = rmsnorm(h, attn_norm[l], BF16)
        if l < n_a:
            qkv = matmul(xn, w_qkv_a[l].astype(BF16))
            o = sb_attention(qkv, batch, seq)
            h = matmul_residual(o, w_o_a[l].astype(BF16), h, 1.0)
        else:
            j = l - n_a
            lam_init = 0.8 - 0.6 * math.exp(-0.3 * l)
            q = matmul(xn, w_q_b[j].astype(BF16))
            o = diff_attention(q, kv_shared, lambda_q1[j], lambda_k1[j], lambda_q2[j],
                               lambda_k2[j], subln_norm[j], batch, seq, lam_init)
            h = matmul_residual(o, w_o_b[j].astype(BF16), h, 1.0)
        h = _macaron_half(h, ffn_norm[l, 1], w_ffn_in[l, 1].astype(BF16),
                          w_ffn_out[l, 1].astype(BF16))
        if l == n_a - 1:
            kv_shared = matmul(rmsnorm(h, kv_norm, BF16), w_kv_b.astype(BF16))
    return rmsnorm(h, final_norm, F32).reshape(batch, seq, d)
```

```python
import functools
import math

import jax
import jax.numpy as jnp
from jax import lax
from jax.experimental import pallas as pl
from jax.experimental.pallas import tpu as pltpu

F32 = jnp.float32
BF16 = jnp.bfloat16

HEAD_DIM = 128
STREAM_CHUNK = 64
RMS_EPS = 1e-6
ATTN_BLOCK = 256
SB_HEADS_PER_STEP = 4
DIFF_HEADS_PER_STEP = 4
LOG2_E = 1.4426950408889634
QK_LOG2_SCALE = LOG2_E / math.sqrt(HEAD_DIM)
EXP2_UNDERFLOW = -150.0
VMEM_LIMIT_BYTES = 56 * 1024 * 1024


def _compiler_params(semantics):
    return pltpu.CompilerParams(dimension_semantics=semantics,
                                vmem_limit_bytes=VMEM_LIMIT_BYTES)


def _pick(n, pref):
    if n <= pref:
        return n
    t = pref
    while n % t:
        t //= 2
    return t


def _rmsnorm_kernel(x_ref, g_ref, o_ref):
    x = x_ref[...]
    ms = jnp.mean(x * x, axis=-1, keepdims=True)
    o_ref[...] = (x * lax.rsqrt(ms + RMS_EPS) * g_ref[...]).astype(o_ref.dtype)


def rmsnorm(x, g, out_dtype):
    m, d = x.shape
    bm = _pick(m, 512)
    return pl.pallas_call(
        _rmsnorm_kernel,
        grid=(m // bm,),
        in_specs=[pl.BlockSpec((bm, d), lambda i: (i, 0)),
                  pl.BlockSpec((1, d), lambda i: (0, 0))],
        out_specs=pl.BlockSpec((bm, d), lambda i: (i, 0)),
        out_shape=jax.ShapeDtypeStruct((m, d), out_dtype),
        compiler_params=_compiler_params(("arbitrary",)),
        name="rmsnorm",
    )(x, g.reshape(1, d))


def _mm_kernel(x_ref, w_ref, o_ref, *, scaled_tiles, factor):
    acc = jnp.dot(x_ref[...], w_ref[...], preferred_element_type=F32)
    if scaled_tiles:
        acc = acc * jnp.where(pl.program_id(1) < scaled_tiles, factor, 1.0)
    o_ref[...] = acc.astype(o_ref.dtype)


def _mm_residual_kernel(x_ref, w_ref, h_ref, o_ref, *, alpha):
    acc = jnp.dot(x_ref[...], w_ref[...], preferred_element_type=F32)
    o_ref[...] = h_ref[...] + alpha * acc


def _mm_swiglu_kernel(x_ref, wg_ref, wu_ref, o_ref):
    x = x_ref[...]
    gate = jnp.dot(x, wg_ref[...], preferred_element_type=F32)
    up = jnp.dot(x, wu_ref[...], preferred_element_type=F32)
    o_ref[...] = (gate * jax.nn.sigmoid(gate) * up).astype(o_ref.dtype)


def matmul(x, w, *, scaled_cols=0, factor=1.0, bm=1024, bn=1024):
    m, k = x.shape
    n = w.shape[1]
    bm, bn = _pick(m, bm), _pick(n, bn)
    assert scaled_cols % bn == 0
    return pl.pallas_call(
        functools.partial(_mm_kernel, scaled_tiles=scaled_cols // bn, factor=factor),
        grid=(m // bm, n // bn),
        in_specs=[pl.BlockSpec((bm, k), lambda i, j: (i, 0)),
                  pl.BlockSpec((k, bn), lambda i, j: (0, j))],
        out_specs=pl.BlockSpec((bm, bn), lambda i, j: (i, j)),
        out_shape=jax.ShapeDtypeStruct((m, n), BF16),
        compiler_params=_compiler_params(("arbitrary", "arbitrary")),
        name="matmul",
    )(x, w)


def matmul_residual(x, w, h, alpha, *, bm=1024, bn=512):
    m, k = x.shape
    n = w.shape[1]
    bm, bn = _pick(m, bm), _pick(n, bn)
    return pl.pallas_call(
        functools.partial(_mm_residual_kernel, alpha=alpha),
        grid=(m // bm, n // bn),
        in_specs=[pl.BlockSpec((bm, k), lambda i, j: (i, 0)),
                  pl.BlockSpec((k, bn), lambda i, j: (0, j)),
                  pl.BlockSpec((bm, bn), lambda i, j: (i, j))],
        out_specs=pl.BlockSpec((bm, bn), lambda i, j: (i, j)),
        out_shape=jax.ShapeDtypeStruct((m, n), F32),
        compiler_params=_compiler_params(("arbitrary", "arbitrary")),
        name="matmul_residual",
    )(x, w, h)


def matmul_swiglu(x, w_in, *, bm=1024, bn=512):
    m, k = x.shape
    f = w_in.shape[1] // 2
    bm, bn = _pick(m, bm), _pick(f, bn)
    nf = f // bn
    return pl.pallas_call(
        _mm_swiglu_kernel,
        grid=(m // bm, nf),
        in_specs=[pl.BlockSpec((bm, k), lambda i, j: (i, 0)),
                  pl.BlockSpec((k, bn), lambda i, j: (0, j)),
                  pl.BlockSpec((k, bn), lambda i, j: (0, nf + j))],
        out_specs=pl.BlockSpec((bm, bn), lambda i, j: (i, j)),
        out_shape=jax.ShapeDtypeStruct((m, f), BF16),
        compiler_params=_compiler_params(("arbitrary", "arbitrary")),
        name="matmul_swiglu",
    )(x, w_in, w_in)


def _sb_attention_kernel(q_ref, k_ref, v_ref, o_ref, kt_ref, acc_ref,
                         *, seq, blk, group):
    nblk = seq // blk
    for g in range(group):
        lanes = slice(g * HEAD_DIM, (g + 1) * HEAD_DIM)
        for b in range(nblk):
            rows = slice(b * blk, (b + 1) * blk)
            kt_ref[g, b] = k_ref[rows, lanes].T

    row = lax.broadcasted_iota(jnp.int32, (blk, blk), 0)
    col = lax.broadcasted_iota(jnp.int32, (blk, blk), 1)
    lower = (row > col).astype(BF16)
    lower2 = jnp.concatenate([lower, lower], axis=0)
    strict = col < row

    def blocks(i, j, carries, masked):
        heads = range(group)
        lanes = [slice(g * HEAD_DIM, (g + 1) * HEAD_DIM) for g in heads]
        q_rows = pl.ds(pl.multiple_of(i * blk, blk), blk)
        k_rows = pl.ds(pl.multiple_of(j * blk, blk), blk)
        zs = [jnp.dot(q_ref[q_rows, lanes[g]], kt_ref[g, j],
                      preferred_element_type=F32) for g in heads]
        log_betas, splits, new_carries = [], [], []
        for g in heads:
            z = zs[g]
            neg_abs = lax.bitcast_convert_type(
                lax.bitcast_convert_type(z, jnp.uint32) | jnp.uint32(0x80000000), F32)
            log_beta = jnp.minimum(z, 0.0) - jnp.log(1.0 + jnp.exp2(neg_abs)) * LOG2_E
            log_1m = log_beta - z
            if masked:
                log_1m = jnp.where(strict, log_1m, 0.0)
            hi = log_1m.astype(BF16)
            lo = (log_1m - hi.astype(F32)).astype(BF16)
            log_betas.append(log_beta)
            splits.append(jnp.concatenate([hi, lo], axis=1))
            new_carries.append(carries[g] + jnp.sum(log_1m, axis=1, keepdims=True))
        laters = [jnp.dot(splits[g], lower2, preferred_element_type=F32) for g in heads]
        weights = []
        for g in heads:
            a = jnp.exp2(log_betas[g] + (laters[g] + carries[g]))
            if masked:
                a = jnp.where(strict, a, 0.0)
            weights.append(a.astype(BF16))
        for g in heads:
            pv = jnp.dot(weights[g], v_ref[k_rows, lanes[g]],
                         preferred_element_type=F32)
            if masked:
                acc_ref[g] = pv
            else:
                acc_ref[g] += pv
        return tuple(new_carries)

    def any_live(carries):
        top = carries[0]
        for c in carries[1:]:
            top = jnp.maximum(top, c)
        return jnp.max(top) > EXP2_UNDERFLOW

    def q_block(i, _):
        zero = jnp.zeros((blk, 1), F32)
        carries = blocks(i, i, (zero,) * group, True)

        def cond(state):
            return jnp.logical_and(state[0] < i, state[1])

        def body(state):
            n = state[0]
            new = blocks(i, i - 1 - n, state[2:], False)
            return (n + 1, any_live(new)) + new

        lax.while_loop(cond, body, (jnp.int32(0), any_live(carries)) + carries)
        out_rows = pl.ds(pl.multiple_of(i * blk, blk), blk)
        for g in range(group):
            o_ref[out_rows, g * HEAD_DIM:(g + 1) * HEAD_DIM] = acc_ref[g].astype(o_ref.dtype)
        return 0

    lax.fori_loop(0, nblk, q_block, 0)


def sb_attention(qkv, batch, seq):
    d = qkv.shape[1] // 3
    heads = d // HEAD_DIM
    group = math.gcd(heads, SB_HEADS_PER_STEP)
    steps = heads // group
    width = group * HEAD_DIM
    blk = _pick(seq, ATTN_BLOCK)
    nblk = seq // blk
    kernel = functools.partial(_sb_attention_kernel, seq=seq, blk=blk, group=group)
    return pl.pallas_call(
        kernel,
        grid=(batch, steps),
        in_specs=[pl.BlockSpec((seq, width), lambda b, h: (b, h)),
                  pl.BlockSpec((seq, width), lambda b, h: (b, steps + h)),
                  pl.BlockSpec((seq, width), lambda b, h: (b, 2 * steps + h))],
        out_specs=pl.BlockSpec((seq, width), lambda b, h: (b, h)),
        out_shape=jax.ShapeDtypeStruct((batch * seq, d), BF16),
        scratch_shapes=[pltpu.VMEM((group, nblk, HEAD_DIM, blk), BF16),
                        pltpu.VMEM((group, blk, HEAD_DIM), F32)],
        compiler_params=_compiler_params(("arbitrary", "arbitrary")),
        name="sb_attention",
    )(qkv, qkv, qkv)


def _diff_attention_kernel(lq1_ref, lk1_ref, lq2_ref, lk2_ref, g_ref, q_ref, k_ref, v_ref,
                           o_ref, qt_ref, vt_ref, acc_ref, bias_ref,
                           *, seq, blk, group, heads, lam_init):
    nblk = seq // blk
    dv = 2 * HEAD_DIM
    for g in range(group):
        for b in range(nblk):
            rows = slice(b * blk, (b + 1) * blk)
            for c in range(2):
                lanes = slice(g * dv + c * HEAD_DIM, g * dv + (c + 1) * HEAD_DIM)
                qt_ref[2 * g + c, b] = q_ref[rows, lanes].T
            vt_ref[g, b] = v_ref[rows, g * dv:(g + 1) * dv].T

    s1 = jnp.sum(lq1_ref[...] * lk1_ref[...], axis=-1, keepdims=True)
    s2 = jnp.sum(lq2_ref[...] * lk2_ref[...], axis=-1, keepdims=True)
    lam = jnp.exp(s1) - jnp.exp(s2) + lam_init

    first_head = jnp.full((1, 1), pl.program_id(1) * group, jnp.int32).astype(F32)
    slopes = [jnp.exp2(-8.0 * (first_head + (g + 1.0)) / heads) * LOG2_E
              for g in range(group)]

    row = lax.broadcasted_iota(jnp.int32, (blk, blk), 0)
    col = lax.broadcasted_iota(jnp.int32, (blk, blk), 1)
    dist = (col - row).astype(F32)
    allowed = (row // STREAM_CHUNK) <= (col // STREAM_CHUNK)
    for g in range(group):
        bias_ref[2 * g] = slopes[g] * dist
        bias_ref[2 * g + 1] = jnp.where(allowed, slopes[g] * jnp.abs(dist), jnp.inf)

    nset = 2 * group

    def update(i, j, state, masked):
        sets = range(nset)
        k_rows = pl.ds(pl.multiple_of(j * blk, blk), blk)
        offset = ((i - j) * blk).astype(F32)
        zs = []
        for u in sets:
            g, c = divmod(u, 2)
            lanes = slice(g * dv + c * HEAD_DIM, g * dv + (c + 1) * HEAD_DIM)
            zs.append(jnp.dot(k_ref[k_rows, lanes], qt_ref[u, i],
                              preferred_element_type=F32))
        new_state, alphas, ps = (), [], []
        for u in sets:
            m, l = state[2 * u], state[2 * u + 1]
            if masked:
                y = zs[u] - bias_ref[2 * (u // 2) + 1]
                m_new = jnp.maximum(m, jnp.max(y, axis=0, keepdims=True))
                shift = m_new
            else:
                y = zs[u] - bias_ref[2 * (u // 2)]
                far = slopes[u // 2] * offset
                m_new = jnp.maximum(m, jnp.max(y, axis=0, keepdims=True) - far)
                shift = m_new + far
            alpha = jnp.exp2(m - m_new)
            p = jnp.exp2(y - shift)
            new_state += (m_new, l * alpha + jnp.sum(p, axis=0, keepdims=True))
            alphas.append(alpha)
            ps.append(p.astype(BF16))
        for u in sets:
            pv = jnp.dot(vt_ref[u // 2, j], ps[u], preferred_element_type=F32)
            if masked:
                acc_ref[u] = pv
            else:
                acc_ref[u] = acc_ref[u] * alphas[u] + pv
        return new_state

    def q_block(i, _):
        neg = jnp.full((1, blk), -jnp.inf, F32)
        zero = jnp.zeros((1, blk), F32)
        state = update(i, i, (neg, zero) * nset, True)

        def body(n, st):
            return update(i, i - 1 - n, st, False)

        state = lax.fori_loop(0, i, body, state)
        out_rows = pl.ds(pl.multiple_of(i * blk, blk), blk)
        for g in range(group):
            l1, l2 = state[4 * g + 1], state[4 * g + 3]
            ot = acc_ref[2 * g] / l1 - lam * (acc_ref[2 * g + 1] / l2)
            o = ot.T
            ms = jnp.mean(o * o, axis=-1, keepdims=True)
            y = o * lax.rsqrt(ms + RMS_EPS) * g_ref[...] * (1.0 - lam_init)
            o_ref[out_rows, g * dv:(g + 1) * dv] = y.astype(o_ref.dtype)
        return 0

    lax.fori_loop(0, nblk, q_block, 0)


def diff_attention(q, kv, lq1, lk1, lq2, lk2, gain, batch, seq, lam_init):
    d = q.shape[1]
    dv = 2 * HEAD_DIM
    heads = d // dv
    group = math.gcd(heads, DIFF_HEADS_PER_STEP)
    steps = heads // group
    width = group * dv
    blk = _pick(seq, ATTN_BLOCK)
    nblk = seq // blk
    kernel = functools.partial(_diff_attention_kernel, seq=seq, blk=blk, group=group,
                               heads=heads, lam_init=lam_init)
    vec = pl.BlockSpec((1, HEAD_DIM), lambda b, h: (0, 0))
    head_blk = pl.BlockSpec((seq, width), lambda b, h: (b, h))
    return pl.pallas_call(
        kernel,
        grid=(batch, steps),
        in_specs=[vec, vec, vec, vec,
                  pl.BlockSpec((1, dv), lambda b, h: (0, 0)),
                  head_blk, head_blk,
                  pl.BlockSpec((seq, width), lambda b, h: (b, steps + h))],
        out_specs=head_blk,
        out_shape=jax.ShapeDtypeStruct((batch * seq, d), BF16),
        scratch_shapes=[pltpu.VMEM((2 * group, nblk, HEAD_DIM, blk), BF16),
                        pltpu.VMEM((group, nblk, dv, blk), BF16),
                        pltpu.VMEM((2 * group, dv, blk), F32),
                        pltpu.VMEM((2 * group, blk, blk), F32)],
        compiler_params=_compiler_params(("arbitrary", "arbitrary")),
        name="diff_attention",
    )(lq1.reshape(1, -1), lk1.reshape(1, -1), lq2.reshape(1, -1), lk2.reshape(1, -1),
      gain.reshape(1, -1), q, kv, kv)


def _macaron_half(h, g, w_in, w_out):
    xn = rmsnorm(h, g, BF16)
    hid = matmul_swiglu(xn, w_in)
    return matmul_residual(hid, w_out, h, 0.5)


def kernel(x, ffn_norm, w_ffn_in, w_ffn_out, attn_norm, w_qkv_a, w_o_a, kv_norm, w_kv_b, w_q_b,
           lambda_q1, lambda_k1, lambda_q2, lambda_k2, subln_norm, w_o_b, final_norm):
    batch, seq, d = x.shape
    depth = ffn_norm.shape[0]
    n_a = w_qkv_a.shape[0]
    h = x.reshape(batch * seq, d)
    kv_shared = None
    for l in range(depth):
        h = _macaron_half(h, ffn_norm[l, 0], w_ffn_in[l, 0].astype(BF16),
                          w_ffn_out[l, 0].astype(BF16))
        xn = rmsnorm(h, attn_norm[l], BF16)
        if l < n_a:
            qkv = matmul(xn, w_qkv_a[l].astype(BF16), scaled_cols=d, factor=QK_LOG2_SCALE)
            o = sb_attention(qkv, batch, seq)
            h = matmul_residual(o, w_o_a[l].astype(BF16), h, 1.0)
        else:
            j = l - n_a
            lam_init = 0.8 - 0.6 * math.exp(-0.3 * l)
            q = matmul(xn, w_q_b[j].astype(BF16), scaled_cols=d, factor=QK_LOG2_SCALE)
            o = diff_attention(q, kv_shared, lambda_q1[j], lambda_k1[j], lambda_q2[j],
                               lambda_k2[j], subln_norm[j], batch, seq, lam_init)
            h = matmul_residual(o, w_o_b[j].astype(BF16), h, 1.0)
        h = _macaron_half(h, ffn_norm[l, 1], w_ffn_in[l, 1].astype(BF16),
                          w_ffn_out[l, 1].astype(BF16))
        if l == n_a - 1:
            kv_shared = matmul(rmsnorm(h, kv_norm, BF16), w_kv_b.astype(BF16))
    return rmsnorm(h, final_norm, F32).reshape(batch, seq, d)
```

```python
import functools
import math

import jax
import jax.numpy as jnp
from jax import lax
from jax.experimental import pallas as pl
from jax.experimental.pallas import tpu as pltpu

F32 = jnp.float32
BF16 = jnp.bfloat16

HEAD_DIM = 128
STREAM_CHUNK = 64
RMS_EPS = 1e-6
ATTN_BLOCK = 256
SB_HEADS_PER_STEP = 4
DIFF_HEADS_PER_STEP = 4
LOG2_E = 1.4426950408889634
QK_LOG2_SCALE = LOG2_E / math.sqrt(HEAD_DIM)
EXP2_UNDERFLOW = -150.0
VMEM_LIMIT_BYTES = 56 * 1024 * 1024
VMEM_TILE_BUDGET = 48 * 1024 * 1024
SSQ_LANES = 128


def _compiler_params(semantics):
    return pltpu.CompilerParams(dimension_semantics=semantics,
                                vmem_limit_bytes=VMEM_LIMIT_BYTES)


def _pick(n, pref):
    if n <= pref:
        return n
    t = pref
    while n % t:
        t //= 2
    return t


def _pick_bn(n, k, bm, weight_tiles, tile_bytes):
    for bn in (1024, 512, 256, 128):
        if n % bn:
            continue
        need = (2 * bm * k * 2 + weight_tiles * (k * bn * (2 * 4 + 2) + bm * bn * 4)
                + 2 * bm * bn * tile_bytes)
        if need <= VMEM_TILE_BUDGET:
            return bn
    return n if n < 128 else 128


def _rmsnorm_kernel(x_ref, g_ref, o_ref):
    x = x_ref[...]
    ms = jnp.mean(x * x, axis=-1, keepdims=True)
    o_ref[...] = (x * lax.rsqrt(ms + RMS_EPS) * g_ref[...]).astype(o_ref.dtype)


def rmsnorm(x, g, out_dtype):
    m, d = x.shape
    bm = _pick(m, 512)
    return pl.pallas_call(
        _rmsnorm_kernel,
        grid=(m // bm,),
        in_specs=[pl.BlockSpec((bm, d), lambda i: (i, 0)),
                  pl.BlockSpec((1, d), lambda i: (0, 0))],
        out_specs=pl.BlockSpec((bm, d), lambda i: (i, 0)),
        out_shape=jax.ShapeDtypeStruct((m, d), out_dtype),
        compiler_params=_compiler_params(("arbitrary",)),
        name="rmsnorm",
    )(x, g.reshape(1, d))


def _row_scale(ssq_ref, width):
    return lax.rsqrt(jnp.sum(ssq_ref[...], axis=-1, keepdims=True) * (1.0 / width) + RMS_EPS)


def _weight_spec(w, index, k, bn, col_offset=0):
    lead = tuple(index)
    return pl.BlockSpec((None,) * len(lead) + (k, bn),
                        lambda i, j: lead + (0, col_offset + j))


def _mm_kernel(*refs, scaled_tiles, factor, normed):
    if normed:
        x_ref, ssq_ref, w_ref, o_ref = refs
    else:
        x_ref, w_ref, o_ref = refs
    acc = jnp.dot(x_ref[...], w_ref[...].astype(BF16), preferred_element_type=F32)
    if normed:
        acc = acc * _row_scale(ssq_ref, x_ref.shape[1])
    if scaled_tiles:
        acc = acc * jnp.where(pl.program_id(1) < scaled_tiles, factor, 1.0)
    o_ref[...] = acc.astype(o_ref.dtype)


def matmul(x, ssq, w, w_index=(), *, scaled_cols=0, factor=1.0, bm=1024):
    m, k = x.shape
    n = w.shape[-1]
    bm = _pick(m, bm)
    bn = _pick_bn(n, k, bm, 1, 2)
    assert scaled_cols % bn == 0
    normed = ssq is not None
    operands = (x, ssq, w) if normed else (x, w)
    in_specs = [pl.BlockSpec((bm, k), lambda i, j: (i, 0))]
    if normed:
        in_specs.append(pl.BlockSpec((bm, SSQ_LANES), lambda i, j: (i, 0)))
    in_specs.append(_weight_spec(w, w_index, k, bn))
    return pl.pallas_call(
        functools.partial(_mm_kernel, scaled_tiles=scaled_cols // bn, factor=factor,
                          normed=normed),
        grid=(m // bm, n // bn),
        in_specs=in_specs,
        out_specs=pl.BlockSpec((bm, bn), lambda i, j: (i, j)),
        out_shape=jax.ShapeDtypeStruct((m, n), BF16),
        compiler_params=_compiler_params(("arbitrary", "arbitrary")),
        name="matmul",
    )(*operands)


def _mm_swiglu_kernel(*refs, normed):
    if normed:
        x_ref, ssq_ref, wg_ref, wu_ref, o_ref = refs
    else:
        x_ref, wg_ref, wu_ref, o_ref = refs
    x = x_ref[...]
    gate = jnp.dot(x, wg_ref[...].astype(BF16), preferred_element_type=F32)
    up = jnp.dot(x, wu_ref[...].astype(BF16), preferred_element_type=F32)
    if normed:
        r = _row_scale(ssq_ref, x_ref.shape[1])
        gate = gate * r
        up = up * r
    o_ref[...] = (gate * jax.nn.sigmoid(gate) * up).astype(o_ref.dtype)


def matmul_swiglu(x, ssq, w_in, w_index, *, bm=1024):
    m, k = x.shape
    f = w_in.shape[-1] // 2
    bm = _pick(m, bm)
    bn = _pick_bn(f, k, bm, 2, 2)
    nf = f // bn
    normed = ssq is not None
    operands = (x, ssq, w_in, w_in) if normed else (x, w_in, w_in)
    in_specs = [pl.BlockSpec((bm, k), lambda i, j: (i, 0))]
    if normed:
        in_specs.append(pl.BlockSpec((bm, SSQ_LANES), lambda i, j: (i, 0)))
    in_specs += [_weight_spec(w_in, w_index, k, bn), _weight_spec(w_in, w_index, k, bn, nf)]
    return pl.pallas_call(
        functools.partial(_mm_swiglu_kernel, normed=normed),
        grid=(m // bm, nf),
        in_specs=in_specs,
        out_specs=pl.BlockSpec((bm, bn), lambda i, j: (i, j)),
        out_shape=jax.ShapeDtypeStruct((m, f), BF16),
        compiler_params=_compiler_params(("arbitrary", "arbitrary")),
        name="matmul_swiglu",
    )(*operands)


def _mm_residual_kernel(*refs, alpha, emit_norm):
    if emit_norm:
        x_ref, w_ref, h_ref, g_ref, o_ref, og_ref, ssq_ref = refs
    else:
        x_ref, w_ref, h_ref, o_ref = refs
    acc = jnp.dot(x_ref[...], w_ref[...].astype(BF16), preferred_element_type=F32)
    out = h_ref[...] + alpha * acc
    o_ref[...] = out
    if emit_norm:
        og_ref[...] = (out * g_ref[...]).astype(og_ref.dtype)
        sq = out * out
        part = sq[:, :SSQ_LANES]
        for c in range(1, out.shape[1] // SSQ_LANES):
            part = part + sq[:, c * SSQ_LANES:(c + 1) * SSQ_LANES]

        @pl.when(pl.program_id(1) == 0)
        def _():
            ssq_ref[...] = part

        @pl.when(pl.program_id(1) != 0)
        def _():
            ssq_ref[...] += part


def matmul_residual(x, w, w_index, h, alpha, gain=None, *, bm=1024):
    m, k = x.shape
    n = w.shape[-1]
    bm = _pick(m, bm)
    bn = _pick_bn(n, k, bm, 1, 4 + 4 + 2)
    emit_norm = gain is not None
    tile = pl.BlockSpec((bm, bn), lambda i, j: (i, j))
    in_specs = [pl.BlockSpec((bm, k), lambda i, j: (i, 0)), _weight_spec(w, w_index, k, bn), tile]
    operands = [x, w, h]
    out_specs = [tile]
    out_shape = [jax.ShapeDtypeStruct((m, n), F32)]
    if emit_norm:
        assert bn % SSQ_LANES == 0
        in_specs.append(pl.BlockSpec((1, bn), lambda i, j: (0, j)))
        operands.append(gain.reshape(1, n))
        out_specs += [tile, pl.BlockSpec((bm, SSQ_LANES), lambda i, j: (i, 0))]
        out_shape += [jax.ShapeDtypeStruct((m, n), BF16),
                      jax.ShapeDtypeStruct((m, SSQ_LANES), F32)]
    outs = pl.pallas_call(
        functools.partial(_mm_residual_kernel, alpha=alpha, emit_norm=emit_norm),
        grid=(m // bm, n // bn),
        in_specs=in_specs,
        out_specs=out_specs,
        out_shape=out_shape,
        compiler_params=_compiler_params(("arbitrary", "arbitrary")),
        name="matmul_residual",
    )(*operands)
    return tuple(outs) if emit_norm else (outs[0], None, None)


def _sb_attention_kernel(q_ref, k_ref, v_ref, o_ref, kt_ref, acc_ref,
                         *, seq, blk, group):
    nblk = seq // blk
    for g in range(group):
        lanes = slice(g * HEAD_DIM, (g + 1) * HEAD_DIM)
        for b in range(nblk):
            rows = slice(b * blk, (b + 1) * blk)
            kt_ref[g, b] = k_ref[rows, lanes].T

    row = lax.broadcasted_iota(jnp.int32, (blk, blk), 0)
    col = lax.broadcasted_iota(jnp.int32, (blk, blk), 1)
    lower = (row > col).astype(BF16)
    lower2 = jnp.concatenate([lower, lower], axis=0)
    strict = col < row

    def blocks(i, j, carries, masked):
        heads = range(group)
        lanes = [slice(g * HEAD_DIM, (g + 1) * HEAD_DIM) for g in heads]
        q_rows = pl.ds(pl.multiple_of(i * blk, blk), blk)
        k_rows = pl.ds(pl.multiple_of(j * blk, blk), blk)
        zs = [jnp.dot(q_ref[q_rows, lanes[g]], kt_ref[g, j],
                      preferred_element_type=F32) for g in heads]
        log_betas, splits, new_carries = [], [], []
        for g in heads:
            z = zs[g]
            neg_abs = lax.bitcast_convert_type(
                lax.bitcast_convert_type(z, jnp.uint32) | jnp.uint32(0x80000000), F32)
            log_beta = jnp.minimum(z, 0.0) - jnp.log(1.0 + jnp.exp2(neg_abs)) * LOG2_E
            log_1m = log_beta - z
            if masked:
                log_1m = jnp.where(strict, log_1m, 0.0)
            hi = log_1m.astype(BF16)
            lo = (log_1m - hi.astype(F32)).astype(BF16)
            log_betas.append(log_beta)
            splits.append(jnp.concatenate([hi, lo], axis=1))
            new_carries.append(carries[g] + jnp.sum(log_1m, axis=1, keepdims=True))
        laters = [jnp.dot(splits[g], lower2, preferred_element_type=F32) for g in heads]
        weights = []
        for g in heads:
            a = jnp.exp2(log_betas[g] + (laters[g] + carries[g]))
            if masked:
                a = jnp.where(strict, a, 0.0)
            weights.append(a.astype(BF16))
        for g in heads:
            pv = jnp.dot(weights[g], v_ref[k_rows, lanes[g]],
                         preferred_element_type=F32)
            if masked:
                acc_ref[g] = pv
            else:
                acc_ref[g] += pv
        return tuple(new_carries)

    def any_live(carries):
        top = carries[0]
        for c in carries[1:]:
            top = jnp.maximum(top, c)
        return jnp.max(top) > EXP2_UNDERFLOW

    def q_block(i, _):
        zero = jnp.zeros((blk, 1), F32)
        carries = blocks(i, i, (zero,) * group, True)

        def cond(state):
            return jnp.logical_and(state[0] < i, state[1])

        def body(state):
            n = state[0]
            new = blocks(i, i - 1 - n, state[2:], False)
            return (n + 1, any_live(new)) + new

        lax.while_loop(cond, body, (jnp.int32(0), any_live(carries)) + carries)
        out_rows = pl.ds(pl.multiple_of(i * blk, blk), blk)
        for g in range(group):
            o_ref[out_rows, g * HEAD_DIM:(g + 1) * HEAD_DIM] = acc_ref[g].astype(o_ref.dtype)
        return 0

    lax.fori_loop(0, nblk, q_block, 0)


def sb_attention(qkv, batch, seq):
    d = qkv.shape[1] // 3
    heads = d // HEAD_DIM
    group = math.gcd(heads, SB_HEADS_PER_STEP)
    steps = heads // group
    width = group * HEAD_DIM
    blk = _pick(seq, ATTN_BLOCK)
    nblk = seq // blk
    kernel = functools.partial(_sb_attention_kernel, seq=seq, blk=blk, group=group)
    return pl.pallas_call(
        kernel,
        grid=(batch, steps),
        in_specs=[pl.BlockSpec((seq, width), lambda b, h: (b, h)),
                  pl.BlockSpec((seq, width), lambda b, h: (b, steps + h)),
                  pl.BlockSpec((seq, width), lambda b, h: (b, 2 * steps + h))],
        out_specs=pl.BlockSpec((seq, width), lambda b, h: (b, h)),
        out_shape=jax.ShapeDtypeStruct((batch * seq, d), BF16),
        scratch_shapes=[pltpu.VMEM((group, nblk, HEAD_DIM, blk), BF16),
                        pltpu.VMEM((group, blk, HEAD_DIM), F32)],
        compiler_params=_compiler_params(("arbitrary", "arbitrary")),
        name="sb_attention",
    )(qkv, qkv, qkv)


def _diff_attention_kernel(lq1_ref, lk1_ref, lq2_ref, lk2_ref, g_ref, q_ref, k_ref, v_ref,
                           o_ref, qt_ref, vt_ref, acc_ref, bias_ref,
                           *, seq, blk, group, heads, lam_init):
    nblk = seq // blk
    dv = 2 * HEAD_DIM
    for g in range(group):
        for b in range(nblk):
            rows = slice(b * blk, (b + 1) * blk)
            for c in range(2):
                lanes = slice(g * dv + c * HEAD_DIM, g * dv + (c + 1) * HEAD_DIM)
                qt_ref[2 * g + c, b] = q_ref[rows, lanes].T
            vt_ref[g, b] = v_ref[rows, g * dv:(g + 1) * dv].T

    s1 = jnp.sum(lq1_ref[...] * lk1_ref[...], axis=-1, keepdims=True)
    s2 = jnp.sum(lq2_ref[...] * lk2_ref[...], axis=-1, keepdims=True)
    lam = jnp.exp(s1) - jnp.exp(s2) + lam_init

    first_head = jnp.full((1, 1), pl.program_id(1) * group, jnp.int32).astype(F32)
    slopes = [jnp.exp2(-8.0 * (first_head + (g + 1.0)) / heads) * LOG2_E
              for g in range(group)]

    row = lax.broadcasted_iota(jnp.int32, (blk, blk), 0)
    col = lax.broadcasted_iota(jnp.int32, (blk, blk), 1)
    dist = (col - row).astype(F32)
    allowed = (row // STREAM_CHUNK) <= (col // STREAM_CHUNK)
    for g in range(group):
        bias_ref[2 * g] = slopes[g] * dist
        bias_ref[2 * g + 1] = jnp.where(allowed, slopes[g] * jnp.abs(dist), jnp.inf)

    nset = 2 * group

    def update(i, j, state, masked):
        sets = range(nset)
        k_rows = pl.ds(pl.multiple_of(j * blk, blk), blk)
        offset = jnp.asarray((i - j) * blk, F32)
        zs = []
        for u in sets:
            g, c = divmod(u, 2)
            lanes = slice(g * dv + c * HEAD_DIM, g * dv + (c + 1) * HEAD_DIM)
            zs.append(jnp.dot(k_ref[k_rows, lanes], qt_ref[u, i],
                              preferred_element_type=F32))
        new_state, alphas, ps = (), [], []
        for u in sets:
            m, l = state[2 * u], state[2 * u + 1]
            if masked:
                y = zs[u] - bias_ref[2 * (u // 2) + 1]
                m_new = jnp.maximum(m, jnp.max(y, axis=0, keepdims=True))
                shift = m_new
            else:
                y = zs[u] - bias_ref[2 * (u // 2)]
                far = slopes[u // 2] * offset
                m_new = jnp.maximum(m, jnp.max(y, axis=0, keepdims=True) - far)
                shift = m_new + far
            alpha = jnp.exp2(m - m_new)
            p = jnp.exp2(y - shift)
            new_state += (m_new, l * alpha + jnp.sum(p, axis=0, keepdims=True))
            alphas.append(alpha)
            ps.append(p.astype(BF16))
        for u in sets:
            pv = jnp.dot(vt_ref[u // 2, j], ps[u], preferred_element_type=F32)
            if masked:
                acc_ref[u] = pv
            else:
                acc_ref[u] = acc_ref[u] * alphas[u] + pv
        return new_state

    def q_block(i, _):
        neg = jnp.full((1, blk), -jnp.inf, F32)
        zero = jnp.zeros((1, blk), F32)
        state = update(i, i, (neg, zero) * nset, True)

        def body(n, st):
            return update(i, i - 1 - n, st, False)

        state = lax.fori_loop(0, i, body, state)
        out_rows = pl.ds(pl.multiple_of(i * blk, blk), blk)
        for g in range(group):
            l1, l2 = state[4 * g + 1], state[4 * g + 3]
            ot = acc_ref[2 * g] / l1 - lam * (acc_ref[2 * g + 1] / l2)
            o = ot.T
            ms = jnp.mean(o * o, axis=-1, keepdims=True)
            y = o * lax.rsqrt(ms + RMS_EPS) * g_ref[...] * (1.0 - lam_init)
            o_ref[out_rows, g * dv:(g + 1) * dv] = y.astype(o_ref.dtype)
        return 0

    lax.fori_loop(0, nblk, q_block, 0)


def diff_attention(q, kv, lq1, lk1, lq2, lk2, gain, batch, seq, lam_init):
    d = q.shape[1]
    dv = 2 * HEAD_DIM
    heads = d // dv
    group = math.gcd(heads, DIFF_HEADS_PER_STEP)
    steps = heads // group
    width = group * dv
    blk = _pick(seq, ATTN_BLOCK)
    nblk = seq // blk
    kernel = functools.partial(_diff_attention_kernel, seq=seq, blk=blk, group=group,
                               heads=heads, lam_init=lam_init)
    vec = pl.BlockSpec((1, HEAD_DIM), lambda b, h: (0, 0))
    head_blk = pl.BlockSpec((seq, width), lambda b, h: (b, h))
    return pl.pallas_call(
        kernel,
        grid=(batch, steps),
        in_specs=[vec, vec, vec, vec,
                  pl.BlockSpec((1, dv), lambda b, h: (0, 0)),
                  head_blk, head_blk,
                  pl.BlockSpec((seq, width), lambda b, h: (b, steps + h))],
        out_specs=head_blk,
        out_shape=jax.ShapeDtypeStruct((batch * seq, d), BF16),
        scratch_shapes=[pltpu.VMEM((2 * group, nblk, HEAD_DIM, blk), BF16),
                        pltpu.VMEM((group, nblk, dv, blk), BF16),
                        pltpu.VMEM((2 * group, dv, blk), F32),
                        pltpu.VMEM((2 * group, blk, blk), F32)],
        compiler_params=_compiler_params(("arbitrary", "arbitrary")),
        name="diff_attention",
    )(lq1.reshape(1, -1), lk1.reshape(1, -1), lq2.reshape(1, -1), lk2.reshape(1, -1),
      gain.reshape(1, -1), q, kv, kv)


def kernel(x, ffn_norm, w_ffn_in, w_ffn_out, attn_norm, w_qkv_a, w_o_a, kv_norm, w_kv_b, w_q_b,
           lambda_q1, lambda_k1, lambda_q2, lambda_k2, subln_norm, w_o_b, final_norm):
    batch, seq, d = x.shape
    depth = ffn_norm.shape[0]
    n_a = w_qkv_a.shape[0]
    h = x.reshape(batch * seq, d)
    xg, ssq = rmsnorm(h, ffn_norm[0, 0], BF16), None
    kv_shared = None
    for l in range(depth):
        hid = matmul_swiglu(xg, ssq, w_ffn_in, (l, 0))
        h, xg, ssq = matmul_residual(hid, w_ffn_out, (l, 0), h, 0.5, attn_norm[l])
        if l < n_a:
            qkv = matmul(xg, ssq, w_qkv_a, (l,), scaled_cols=d, factor=QK_LOG2_SCALE)
            o = sb_attention(qkv, batch, seq)
            h, xg, ssq = matmul_residual(o, w_o_a, (l,), h, 1.0, ffn_norm[l, 1])
        else:
            j = l - n_a
            lam_init = 0.8 - 0.6 * math.exp(-0.3 * l)
            q = matmul(xg, ssq, w_q_b, (j,), scaled_cols=d, factor=QK_LOG2_SCALE)
            o = diff_attention(q, kv_shared, lambda_q1[j], lambda_k1[j], lambda_q2[j],
                               lambda_k2[j], subln_norm[j], batch, seq, lam_init)
            h, xg, ssq = matmul_residual(o, w_o_b, (j,), h, 1.0, ffn_norm[l, 1])
        hid = matmul_swiglu(xg, ssq, w_ffn_in, (l, 1))
        next_gain = ffn_norm[l + 1, 0] if l + 1 < depth else None
        h, xg, ssq = matmul_residual(hid, w_ffn_out, (l, 1), h, 0.5, next_gain)
        if l == n_a - 1:
            kv_shared = matmul(rmsnorm(h, kv_norm, BF16), None, w_kv_b)
    return rmsnorm(h, final_norm, F32).reshape(batch, seq, d)
```

```python
import functools
import math

import jax
import jax.numpy as jnp
from jax import lax
from jax.experimental import pallas as pl
from jax.experimental.pallas import tpu as pltpu

F32 = jnp.float32
BF16 = jnp.bfloat16

HEAD_DIM = 128
STREAM_CHUNK = 64
RMS_EPS = 1e-6
ATTN_BLOCK = 256
SB_HEADS_PER_STEP = 4
DIFF_HEADS_PER_STEP = 4
LOG2_E = 1.4426950408889634
QK_LOG2_SCALE = LOG2_E / math.sqrt(HEAD_DIM)
EXP2_UNDERFLOW = -150.0
VMEM_LIMIT_BYTES = 56 * 1024 * 1024
VMEM_TILE_BUDGET = 51 * 1024 * 1024
SSQ_LANES = 128


def _compiler_params(semantics):
    return pltpu.CompilerParams(dimension_semantics=semantics,
                                vmem_limit_bytes=VMEM_LIMIT_BYTES)


def _pick(n, pref):
    if n <= pref:
        return n
    t = pref
    while n % t:
        t //= 2
    return t


def _pick_bn(n, k, bm, weight_tiles, tile_bytes, extra_bytes):
    for bn in (1024, 512, 256, 128):
        if n % bn:
            continue
        need = (2 * bm * k * 2 + weight_tiles * (2 * k * bn * 2 + bm * bn * 4)
                + 2 * bm * bn * tile_bytes + extra_bytes(bn))
        if need <= VMEM_TILE_BUDGET:
            return bn
    return n if n < 128 else 128


def _rmsnorm_kernel(x_ref, g_ref, o_ref):
    x = x_ref[...]
    ms = jnp.mean(x * x, axis=-1, keepdims=True)
    o_ref[...] = (x * lax.rsqrt(ms + RMS_EPS) * g_ref[...]).astype(o_ref.dtype)


def rmsnorm(x, g, out_dtype):
    m, d = x.shape
    bm = _pick(m, 512)
    return pl.pallas_call(
        _rmsnorm_kernel,
        grid=(m // bm,),
        in_specs=[pl.BlockSpec((bm, d), lambda i: (i, 0)),
                  pl.BlockSpec((1, d), lambda i: (0, 0))],
        out_specs=pl.BlockSpec((bm, d), lambda i: (i, 0)),
        out_shape=jax.ShapeDtypeStruct((m, d), out_dtype),
        compiler_params=_compiler_params(("arbitrary",)),
        name="rmsnorm",
    )(x, g.reshape(1, d))


def _row_scale(ssq_ref, width):
    return lax.rsqrt(jnp.sum(ssq_ref[...], axis=-1, keepdims=True) * (1.0 / width) + RMS_EPS)


def _weight_spec(w, index, k, bn, col_offset=0):
    lead = tuple(index)
    return pl.BlockSpec((None,) * len(lead) + (k, bn),
                        lambda i, j: lead + (0, col_offset + j))


def _cast_slab_rows(k, steps):
    return next(r for r in range(16, k + 1, 16) if k % r == 0 and k // r <= steps)


def _cast_job_bytes(jobs, steps):
    return sum(2 * _cast_slab_rows(w.shape[-2], steps) * w.shape[-1] * (4 + 2) for w, _ in jobs)


def _cast_job_specs(jobs, grid):
    steps = grid[0] * grid[1]
    in_specs, out_specs, out_shape = [], [], []
    for w, index in jobs:
        k, n = w.shape[-2:]
        rb = _cast_slab_rows(k, steps)
        lead = tuple(index)

        def slab(i, j, last=k // rb - 1):
            return jnp.minimum(i * grid[1] + j, last)

        in_specs.append(pl.BlockSpec((None,) * len(lead) + (rb, n),
                                     lambda i, j, lead=lead, slab=slab: lead + (slab(i, j), 0)))
        out_specs.append(pl.BlockSpec((rb, n), lambda i, j, slab=slab: (slab(i, j), 0)))
        out_shape.append(jax.ShapeDtypeStruct((k, n), BF16))
    return in_specs, out_specs, out_shape


def _run_cast_jobs(src_refs, dst_refs):
    for src, dst in zip(src_refs, dst_refs):
        dst[...] = src[...].astype(dst.dtype)


def _split_refs(refs, n_in, n_out, n_jobs):
    a, b, c = n_in, n_in + n_jobs, n_in + n_jobs + n_out
    return refs[:a], refs[a:b], refs[b:c], refs[c:]


def _mm_kernel(*refs, scaled_tiles, factor, normed, n_jobs):
    ins, job_src, (o_ref,), job_dst = _split_refs(refs, 3 if normed else 2, 1, n_jobs)
    x_ref, w_ref = ins[0], ins[-1]
    acc = jnp.dot(x_ref[...], w_ref[...], preferred_element_type=F32)
    if normed:
        acc = acc * _row_scale(ins[1], x_ref.shape[1])
    if scaled_tiles:
        acc = acc * jnp.where(pl.program_id(1) < scaled_tiles, factor, 1.0)
    o_ref[...] = acc.astype(o_ref.dtype)
    _run_cast_jobs(job_src, job_dst)


def matmul(x, ssq, w, w_index=(), *, scaled_cols=0, factor=1.0, jobs=(), bm=1024):
    m, k = x.shape
    n = w.shape[-1]
    bm = _pick(m, bm)
    bn = _pick_bn(n, k, bm, 1, 2, lambda bn: _cast_job_bytes(jobs, (m // bm) * (n // bn)))
    assert scaled_cols % bn == 0
    grid = (m // bm, n // bn)
    normed = ssq is not None
    operands = (x, ssq, w) if normed else (x, w)
    in_specs = [pl.BlockSpec((bm, k), lambda i, j: (i, 0))]
    if normed:
        in_specs.append(pl.BlockSpec((bm, SSQ_LANES), lambda i, j: (i, 0)))
    in_specs.append(_weight_spec(w, w_index, k, bn))
    job_in, job_out, job_shape = _cast_job_specs(jobs, grid)
    outs = pl.pallas_call(
        functools.partial(_mm_kernel, scaled_tiles=scaled_cols // bn, factor=factor,
                          normed=normed, n_jobs=len(jobs)),
        grid=grid,
        in_specs=in_specs + job_in,
        out_specs=[pl.BlockSpec((bm, bn), lambda i, j: (i, j))] + job_out,
        out_shape=[jax.ShapeDtypeStruct((m, n), BF16)] + job_shape,
        compiler_params=_compiler_params(("arbitrary", "arbitrary")),
        name="matmul",
    )(*operands, *[w for w, _ in jobs])
    return outs[0], list(outs[1:])


def _mm_swiglu_kernel(*refs, normed, n_jobs):
    ins, job_src, (o_ref,), job_dst = _split_refs(refs, 4 if normed else 3, 1, n_jobs)
    x_ref, wg_ref, wu_ref = ins[0], ins[-2], ins[-1]
    x = x_ref[...]
    gate = jnp.dot(x, wg_ref[...], preferred_element_type=F32)
    up = jnp.dot(x, wu_ref[...], preferred_element_type=F32)
    if normed:
        r = _row_scale(ins[1], x_ref.shape[1])
        gate = gate * r
        up = up * r
    o_ref[...] = (gate * jax.nn.sigmoid(gate) * up).astype(o_ref.dtype)
    _run_cast_jobs(job_src, job_dst)


def matmul_swiglu(x, ssq, w_in, *, jobs=(), bm=1024):
    m, k = x.shape
    f = w_in.shape[-1] // 2
    bm = _pick(m, bm)
    bn = _pick_bn(f, k, bm, 2, 2, lambda bn: _cast_job_bytes(jobs, (m // bm) * (f // bn)))
    nf = f // bn
    grid = (m // bm, nf)
    normed = ssq is not None
    operands = (x, ssq, w_in, w_in) if normed else (x, w_in, w_in)
    in_specs = [pl.BlockSpec((bm, k), lambda i, j: (i, 0))]
    if normed:
        in_specs.append(pl.BlockSpec((bm, SSQ_LANES), lambda i, j: (i, 0)))
    in_specs += [_weight_spec(w_in, (), k, bn), _weight_spec(w_in, (), k, bn, nf)]
    job_in, job_out, job_shape = _cast_job_specs(jobs, grid)
    outs = pl.pallas_call(
        functools.partial(_mm_swiglu_kernel, normed=normed, n_jobs=len(jobs)),
        grid=grid,
        in_specs=in_specs + job_in,
        out_specs=[pl.BlockSpec((bm, bn), lambda i, j: (i, j))] + job_out,
        out_shape=[jax.ShapeDtypeStruct((m, f), BF16)] + job_shape,
        compiler_params=_compiler_params(("arbitrary", "arbitrary")),
        name="matmul_swiglu",
    )(*operands, *[w for w, _ in jobs])
    return outs[0], list(outs[1:])


def _mm_residual_kernel(*refs, alpha, n_gains, n_jobs):
    ins, job_src, outs, job_dst = _split_refs(refs, 3 + n_gains, 1 + n_gains + bool(n_gains),
                                              n_jobs)
    x_ref, w_ref, h_ref = ins[:3]
    g_refs = ins[3:]
    o_ref = outs[0]
    og_refs = outs[1:1 + n_gains]
    acc = jnp.dot(x_ref[...], w_ref[...], preferred_element_type=F32)
    out = h_ref[...] + alpha * acc
    o_ref[...] = out
    _run_cast_jobs(job_src, job_dst)
    if n_gains:
        ssq_ref = outs[-1]
        for g_ref, og_ref in zip(g_refs, og_refs):
            og_ref[...] = (out * g_ref[...]).astype(og_ref.dtype)
        sq = out * out
        part = sq[:, :SSQ_LANES]
        for c in range(1, out.shape[1] // SSQ_LANES):
            part = part + sq[:, c * SSQ_LANES:(c + 1) * SSQ_LANES]

        @pl.when(pl.program_id(1) == 0)
        def _():
            ssq_ref[...] = part

        @pl.when(pl.program_id(1) != 0)
        def _():
            ssq_ref[...] += part


def matmul_residual(x, w, h, alpha, gains=(), *, jobs=(), bm=1024):
    m, k = x.shape
    n = w.shape[-1]
    bm = _pick(m, bm)
    bn = _pick_bn(n, k, bm, 1, 4 + 4 + 2 * len(gains),
                  lambda bn: _cast_job_bytes(jobs, (m // bm) * (n // bn)))
    grid = (m // bm, n // bn)
    tile = pl.BlockSpec((bm, bn), lambda i, j: (i, j))
    in_specs = [pl.BlockSpec((bm, k), lambda i, j: (i, 0)), _weight_spec(w, (), k, bn), tile]
    in_specs += [pl.BlockSpec((1, bn), lambda i, j: (0, j))] * len(gains)
    out_specs = [tile] * (1 + len(gains))
    out_shape = [jax.ShapeDtypeStruct((m, n), F32)]
    out_shape += [jax.ShapeDtypeStruct((m, n), BF16)] * len(gains)
    if gains:
        assert bn % SSQ_LANES == 0
        out_specs.append(pl.BlockSpec((bm, SSQ_LANES), lambda i, j: (i, 0)))
        out_shape.append(jax.ShapeDtypeStruct((m, SSQ_LANES), F32))
    n_main = len(out_specs)
    job_in, job_out, job_shape = _cast_job_specs(jobs, grid)
    outs = pl.pallas_call(
        functools.partial(_mm_residual_kernel, alpha=alpha, n_gains=len(gains),
                          n_jobs=len(jobs)),
        grid=grid,
        in_specs=in_specs + job_in,
        out_specs=out_specs + job_out,
        out_shape=out_shape + job_shape,
        compiler_params=_compiler_params(("arbitrary", "arbitrary")),
        name="matmul_residual",
    )(x, w, h, *[g.reshape(1, n) for g in gains], *[w for w, _ in jobs])
    ssq = outs[n_main - 1] if gains else None
    return outs[0], list(outs[1:1 + len(gains)]), ssq, list(outs[n_main:])


def _sb_attention_kernel(q_ref, k_ref, v_ref, o_ref, kt_ref, acc_ref,
                         *, seq, blk, group):
    nblk = seq // blk
    for g in range(group):
        lanes = slice(g * HEAD_DIM, (g + 1) * HEAD_DIM)
        for b in range(nblk):
            rows = slice(b * blk, (b + 1) * blk)
            kt_ref[g, b] = k_ref[rows, lanes].T

    row = lax.broadcasted_iota(jnp.int32, (blk, blk), 0)
    col = lax.broadcasted_iota(jnp.int32, (blk, blk), 1)
    lower = (row > col).astype(BF16)
    lower2 = jnp.concatenate([lower, lower], axis=0)
    strict = col < row

    def blocks(i, j, carries, masked):
        heads = range(group)
        lanes = [slice(g * HEAD_DIM, (g + 1) * HEAD_DIM) for g in heads]
        q_rows = pl.ds(pl.multiple_of(i * blk, blk), blk)
        k_rows = pl.ds(pl.multiple_of(j * blk, blk), blk)
        zs = [jnp.dot(q_ref[q_rows, lanes[g]], kt_ref[g, j],
                      preferred_element_type=F32) for g in heads]
        log_betas, splits, new_carries = [], [], []
        for g in heads:
            z = zs[g]
            neg_abs = lax.bitcast_convert_type(
                lax.bitcast_convert_type(z, jnp.uint32) | jnp.uint32(0x80000000), F32)
            log_beta = jnp.minimum(z, 0.0) - jnp.log(1.0 + jnp.exp2(neg_abs)) * LOG2_E
            log_1m = log_beta - z
            if masked:
                log_1m = jnp.where(strict, log_1m, 0.0)
            hi = log_1m.astype(BF16)
            lo = (log_1m - hi.astype(F32)).astype(BF16)
            log_betas.append(log_beta)
            splits.append(jnp.concatenate([hi, lo], axis=1))
            new_carries.append(carries[g] + jnp.sum(log_1m, axis=1, keepdims=True))
        laters = [jnp.dot(splits[g], lower2, preferred_element_type=F32) for g in heads]
        weights = []
        for g in heads:
            a = jnp.exp2(log_betas[g] + (laters[g] + carries[g]))
            if masked:
                a = jnp.where(strict, a, 0.0)
            weights.append(a.astype(BF16))
        for g in heads:
            pv = jnp.dot(weights[g], v_ref[k_rows, lanes[g]],
                         preferred_element_type=F32)
            if masked:
                acc_ref[g] = pv
            else:
                acc_ref[g] += pv
        return tuple(new_carries)

    def any_live(carries):
        top = carries[0]
        for c in carries[1:]:
            top = jnp.maximum(top, c)
        return jnp.max(top) > EXP2_UNDERFLOW

    def q_block(i, _):
        zero = jnp.zeros((blk, 1), F32)
        carries = blocks(i, i, (zero,) * group, True)

        def cond(state):
            return jnp.logical_and(state[0] < i, state[1])

        def body(state):
            n = state[0]
            new = blocks(i, i - 1 - n, state[2:], False)
            return (n + 1, any_live(new)) + new

        lax.while_loop(cond, body, (jnp.int32(0), any_live(carries)) + carries)
        out_rows = pl.ds(pl.multiple_of(i * blk, blk), blk)
        for g in range(group):
            o_ref[out_rows, g * HEAD_DIM:(g + 1) * HEAD_DIM] = acc_ref[g].astype(o_ref.dtype)
        return 0

    lax.fori_loop(0, nblk, q_block, 0)


def sb_attention(qkv, batch, seq):
    d = qkv.shape[1] // 3
    heads = d // HEAD_DIM
    group = math.gcd(heads, SB_HEADS_PER_STEP)
    steps = heads // group
    width = group * HEAD_DIM
    blk = _pick(seq, ATTN_BLOCK)
    nblk = seq // blk
    kernel = functools.partial(_sb_attention_kernel, seq=seq, blk=blk, group=group)
    return pl.pallas_call(
        kernel,
        grid=(batch, steps),
        in_specs=[pl.BlockSpec((seq, width), lambda b, h: (b, h)),
                  pl.BlockSpec((seq, width), lambda b, h: (b, steps + h)),
                  pl.BlockSpec((seq, width), lambda b, h: (b, 2 * steps + h))],
        out_specs=pl.BlockSpec((seq, width), lambda b, h: (b, h)),
        out_shape=jax.ShapeDtypeStruct((batch * seq, d), BF16),
        scratch_shapes=[pltpu.VMEM((group, nblk, HEAD_DIM, blk), BF16),
                        pltpu.VMEM((group, blk, HEAD_DIM), F32)],
        compiler_params=_compiler_params(("arbitrary", "arbitrary")),
        name="sb_attention",
    )(qkv, qkv, qkv)


def _diff_attention_kernel(lq1_ref, lk1_ref, lq2_ref, lk2_ref, g_ref, q_ref, k_ref, v_ref,
                           o_ref, qt_ref, vt_ref, acc_ref, bias_ref,
                           *, seq, blk, group, heads, lam_init):
    nblk = seq // blk
    dv = 2 * HEAD_DIM
    for g in range(group):
        for b in range(nblk):
            rows = slice(b * blk, (b + 1) * blk)
            for c in range(2):
                lanes = slice(g * dv + c * HEAD_DIM, g * dv + (c + 1) * HEAD_DIM)
                qt_ref[2 * g + c, b] = q_ref[rows, lanes].T
            vt_ref[g, b] = v_ref[rows, g * dv:(g + 1) * dv].T

    s1 = jnp.sum(lq1_ref[...] * lk1_ref[...], axis=-1, keepdims=True)
    s2 = jnp.sum(lq2_ref[...] * lk2_ref[...], axis=-1, keepdims=True)
    lam = jnp.exp(s1) - jnp.exp(s2) + lam_init

    first_head = jnp.full((1, 1), pl.program_id(1) * group, jnp.int32).astype(F32)
    slopes = [jnp.exp2(-8.0 * (first_head + (g + 1.0)) / heads) * LOG2_E
              for g in range(group)]

    row = lax.broadcasted_iota(jnp.int32, (blk, blk), 0)
    col = lax.broadcasted_iota(jnp.int32, (blk, blk), 1)
    dist = (col - row).astype(F32)
    allowed = (row // STREAM_CHUNK) <= (col // STREAM_CHUNK)
    for g in range(group):
        bias_ref[2 * g] = slopes[g] * dist
        bias_ref[2 * g + 1] = jnp.where(allowed, slopes[g] * jnp.abs(dist), jnp.inf)

    nset = 2 * group

    def update(i, j, state, masked):
        sets = range(nset)
        k_rows = pl.ds(pl.multiple_of(j * blk, blk), blk)
        offset = jnp.asarray((i - j) * blk, F32)
        zs = []
        for u in sets:
            g, c = divmod(u, 2)
            lanes = slice(g * dv + c * HEAD_DIM, g * dv + (c + 1) * HEAD_DIM)
            zs.append(jnp.dot(k_ref[k_rows, lanes], qt_ref[u, i],
                              preferred_element_type=F32))
        new_state, alphas, ps = (), [], []
        for u in sets:
            m, l = state[2 * u], state[2 * u + 1]
            if masked:
                y = zs[u] - bias_ref[2 * (u // 2) + 1]
                m_new = jnp.maximum(m, jnp.max(y, axis=0, keepdims=True))
                shift = m_new
            else:
                y = zs[u] - bias_ref[2 * (u // 2)]
                far = slopes[u // 2] * offset
                m_new = jnp.maximum(m, jnp.max(y, axis=0, keepdims=True) - far)
                shift = m_new + far
            alpha = jnp.exp2(m - m_new)
            p = jnp.exp2(y - shift)
            new_state += (m_new, l * alpha + jnp.sum(p, axis=0, keepdims=True))
            alphas.append(alpha)
            ps.append(p.astype(BF16))
        for u in sets:
            pv = jnp.dot(vt_ref[u // 2, j], ps[u], preferred_element_type=F32)
            if masked:
                acc_ref[u] = pv
            else:
                acc_ref[u] = acc_ref[u] * alphas[u] + pv
        return new_state

    def q_block(i, _):
        neg = jnp.full((1, blk), -jnp.inf, F32)
        zero = jnp.zeros((1, blk), F32)
        state = update(i, i, (neg, zero) * nset, True)

        def body(n, st):
            return update(i, i - 1 - n, st, False)

        state = lax.fori_loop(0, i, body, state)
        out_rows = pl.ds(pl.multiple_of(i * blk, blk), blk)
        for g in range(group):
            l1, l2 = state[4 * g + 1], state[4 * g + 3]
            ot = acc_ref[2 * g] / l1 - lam * (acc_ref[2 * g + 1] / l2)
            o = ot.T
            ms = jnp.mean(o * o, axis=-1, keepdims=True)
            y = o * lax.rsqrt(ms + RMS_EPS) * g_ref[...] * (1.0 - lam_init)
            o_ref[out_rows, g * dv:(g + 1) * dv] = y.astype(o_ref.dtype)
        return 0

    lax.fori_loop(0, nblk, q_block, 0)


def diff_attention(q, kv, lq1, lk1, lq2, lk2, gain, batch, seq, lam_init):
    d = q.shape[1]
    dv = 2 * HEAD_DIM
    heads = d // dv
    group = math.gcd(heads, DIFF_HEADS_PER_STEP)
    steps = heads // group
    width = group * dv
    blk = _pick(seq, ATTN_BLOCK)
    nblk = seq // blk
    kernel = functools.partial(_diff_attention_kernel, seq=seq, blk=blk, group=group,
                               heads=heads, lam_init=lam_init)
    vec = pl.BlockSpec((1, HEAD_DIM), lambda b, h: (0, 0))
    head_blk = pl.BlockSpec((seq, width), lambda b, h: (b, h))
    return pl.pallas_call(
        kernel,
        grid=(batch, steps),
        in_specs=[vec, vec, vec, vec,
                  pl.BlockSpec((1, dv), lambda b, h: (0, 0)),
                  head_blk, head_blk,
                  pl.BlockSpec((seq, width), lambda b, h: (b, steps + h))],
        out_specs=head_blk,
        out_shape=jax.ShapeDtypeStruct((batch * seq, d), BF16),
        scratch_shapes=[pltpu.VMEM((2 * group, nblk, HEAD_DIM, blk), BF16),
                        pltpu.VMEM((group, nblk, dv, blk), BF16),
                        pltpu.VMEM((2 * group, dv, blk), F32),
                        pltpu.VMEM((2 * group, blk, blk), F32)],
        compiler_params=_compiler_params(("arbitrary", "arbitrary")),
        name="diff_attention",
    )(lq1.reshape(1, -1), lk1.reshape(1, -1), lq2.reshape(1, -1), lk2.reshape(1, -1),
      gain.reshape(1, -1), q, kv, kv)


def kernel(x, ffn_norm, w_ffn_in, w_ffn_out, attn_norm, w_qkv_a, w_o_a, kv_norm, w_kv_b, w_q_b,
           lambda_q1, lambda_k1, lambda_q2, lambda_k2, subln_norm, w_o_b, final_norm):
    batch, seq, d = x.shape
    depth = ffn_norm.shape[0]
    n_a = w_qkv_a.shape[0]
    h = x.reshape(batch * seq, d)
    j_of = lambda l: l - n_a

    ready = {}

    def job(name, *index):
        src = {"in": w_ffn_in, "out": w_ffn_out, "qkv": w_qkv_a, "oa": w_o_a, "kv": w_kv_b,
               "qb": w_q_b, "ob": w_o_b}[name]
        return (name,) + index, (src, index)

    def run(call, keyed_jobs):
        *main, cast = call(jobs=tuple(j for _, j in keyed_jobs))
        ready.update({key: w for (key, _), w in zip(keyed_jobs, cast)})
        return main

    ready[("in", 0, 0)] = w_ffn_in[0, 0].astype(BF16)
    xg, ssq = rmsnorm(h, ffn_norm[0, 0], BF16), None
    kv_shared = None
    for l in range(depth):
        sb_layer = l < n_a
        proj = job("qkv", l) if sb_layer else job("qb", j_of(l))
        out_proj = job("oa", l) if sb_layer else job("ob", j_of(l))
        (hid,) = run(functools.partial(matmul_swiglu, xg, ssq, ready[("in", l, 0)]),
                     [job("out", l, 0), proj])
        h, (xg,), ssq = matmul_residual(hid, ready[("out", l, 0)], h, 0.5, (attn_norm[l],))[:3]
        (qx,) = run(functools.partial(matmul, xg, ssq, ready[proj[0]], scaled_cols=d,
                                      factor=QK_LOG2_SCALE), [out_proj])
        if sb_layer:
            o = sb_attention(qx, batch, seq)
        else:
            lam_init = 0.8 - 0.6 * math.exp(-0.3 * l)
            jj = j_of(l)
            o = diff_attention(qx, kv_shared, lambda_q1[jj], lambda_k1[jj], lambda_q2[jj],
                               lambda_k2[jj], subln_norm[jj], batch, seq, lam_init)
        h, (xg,), ssq = run(functools.partial(matmul_residual, o, ready[out_proj[0]], h, 1.0,
                                              (ffn_norm[l, 1],)), [job("in", l, 1)])
        later = [job("out", l, 1)]
        if l + 1 < depth:
            later.append(job("in", l + 1, 0))
        if l == n_a - 1:
            later.append(job("kv"))
        (hid,) = run(functools.partial(matmul_swiglu, xg, ssq, ready[("in", l, 1)]), later)
        gains = (ffn_norm[l + 1, 0],) if l + 1 < depth else ()
        if l == n_a - 1:
            gains += (kv_norm,)
        h, normed, ssq = matmul_residual(hid, ready[("out", l, 1)], h, 0.5, gains)[:3]
        xg = normed[0] if l + 1 < depth else None
        if l == n_a - 1:
            kv_shared = matmul(normed[-1], ssq, ready[("kv",)])[0]
    return rmsnorm(h, final_norm, F32).reshape(batch, seq, d)
```

```python
import functools
import math

import jax
import jax.numpy as jnp
from jax import lax
from jax.experimental import pallas as pl
from jax.experimental.pallas import tpu as pltpu

F32 = jnp.float32
BF16 = jnp.bfloat16

HEAD_DIM = 128
STREAM_CHUNK = 64
RMS_EPS = 1e-6
ATTN_BLOCK = 256
SB_HEADS_PER_STEP = 4
DIFF_HEADS_PER_STEP = 4
LOG2_E = 1.4426950408889634
QK_LOG2_SCALE = LOG2_E / math.sqrt(HEAD_DIM)
EXP2_UNDERFLOW = -150.0
VMEM_LIMIT_BYTES = 56 * 1024 * 1024
VMEM_TILE_BUDGET = 51 * 1024 * 1024
SSQ_LANES = 128


def _compiler_params(semantics):
    return pltpu.CompilerParams(dimension_semantics=semantics,
                                vmem_limit_bytes=VMEM_LIMIT_BYTES)


def _pick(n, pref):
    if n <= pref:
        return n
    t = pref
    while n % t:
        t //= 2
    return t


def _pick_bn(n, k, bm, weight_tiles, tile_bytes, extra_bytes):
    for bn in (1024, 512, 256, 128):
        if n % bn:
            continue
        need = (2 * bm * k * 2 + weight_tiles * (2 * k * bn * 2 + bm * bn * 4)
                + 2 * bm * bn * tile_bytes + extra_bytes(bn))
        if need <= VMEM_TILE_BUDGET:
            return bn
    return n if n < 128 else 128


def _rmsnorm_kernel(x_ref, g_ref, o_ref):
    x = x_ref[...]
    ms = jnp.mean(x * x, axis=-1, keepdims=True)
    o_ref[...] = (x * lax.rsqrt(ms + RMS_EPS) * g_ref[...]).astype(o_ref.dtype)


def rmsnorm(x, g, out_dtype):
    m, d = x.shape
    bm = _pick(m, 512)
    return pl.pallas_call(
        _rmsnorm_kernel,
        grid=(m // bm,),
        in_specs=[pl.BlockSpec((bm, d), lambda i: (i, 0)),
                  pl.BlockSpec((1, d), lambda i: (0, 0))],
        out_specs=pl.BlockSpec((bm, d), lambda i: (i, 0)),
        out_shape=jax.ShapeDtypeStruct((m, d), out_dtype),
        compiler_params=_compiler_params(("arbitrary",)),
        name="rmsnorm",
    )(x, g.reshape(1, d))


def _row_scale(ssq_ref, width):
    return lax.rsqrt(jnp.sum(ssq_ref[...], axis=-1, keepdims=True) * (1.0 / width) + RMS_EPS)


def _weight_spec(w, index, k, bn, col_offset=0):
    lead = tuple(index)
    return pl.BlockSpec((None,) * len(lead) + (k, bn),
                        lambda i, j: lead + (0, col_offset + j))


def _cast_slab_rows(k, steps):
    return next(r for r in range(16, k + 1, 16) if k % r == 0 and k // r <= steps)


def _cast_job_bytes(jobs, steps):
    return sum(2 * _cast_slab_rows(w.shape[-2], steps) * w.shape[-1] * (4 + 2) for w, _ in jobs)


def _cast_job_specs(jobs, grid):
    steps = grid[0] * grid[1]
    in_specs, out_specs, out_shape = [], [], []
    for w, index in jobs:
        k, n = w.shape[-2:]
        rb = _cast_slab_rows(k, steps)
        lead = tuple(index)

        def slab(i, j, last=k // rb - 1):
            return jnp.minimum(i * grid[1] + j, last)

        in_specs.append(pl.BlockSpec((None,) * len(lead) + (rb, n),
                                     lambda i, j, lead=lead, slab=slab: lead + (slab(i, j), 0)))
        out_specs.append(pl.BlockSpec((rb, n), lambda i, j, slab=slab: (slab(i, j), 0)))
        out_shape.append(jax.ShapeDtypeStruct((k, n), BF16))
    return in_specs, out_specs, out_shape


def _run_cast_jobs(src_refs, dst_refs):
    for src, dst in zip(src_refs, dst_refs):
        dst[...] = src[...].astype(dst.dtype)


def _split_refs(refs, n_in, n_out, n_jobs):
    a, b, c = n_in, n_in + n_jobs, n_in + n_jobs + n_out
    return refs[:a], refs[a:b], refs[b:c], refs[c:]


def _mm_kernel(*refs, scaled_tiles, factor, normed, n_jobs):
    ins, job_src, (o_ref,), job_dst = _split_refs(refs, 3 if normed else 2, 1, n_jobs)
    x_ref, w_ref = ins[0], ins[-1]
    acc = jnp.dot(x_ref[...], w_ref[...], preferred_element_type=F32)
    if normed:
        acc = acc * _row_scale(ins[1], x_ref.shape[1])
    if scaled_tiles:
        acc = acc * jnp.where(pl.program_id(1) < scaled_tiles, factor, 1.0)
    o_ref[...] = acc.astype(o_ref.dtype)
    _run_cast_jobs(job_src, job_dst)


def matmul(x, ssq, w, w_index=(), *, scaled_cols=0, factor=1.0, jobs=(), bm=1024):
    m, k = x.shape
    n = w.shape[-1]
    bm = _pick(m, bm)
    bn = _pick_bn(n, k, bm, 1, 2, lambda bn: _cast_job_bytes(jobs, (m // bm) * (n // bn)))
    assert scaled_cols % bn == 0
    grid = (m // bm, n // bn)
    normed = ssq is not None
    operands = (x, ssq, w) if normed else (x, w)
    in_specs = [pl.BlockSpec((bm, k), lambda i, j: (i, 0))]
    if normed:
        in_specs.append(pl.BlockSpec((bm, SSQ_LANES), lambda i, j: (i, 0)))
    in_specs.append(_weight_spec(w, w_index, k, bn))
    job_in, job_out, job_shape = _cast_job_specs(jobs, grid)
    outs = pl.pallas_call(
        functools.partial(_mm_kernel, scaled_tiles=scaled_cols // bn, factor=factor,
                          normed=normed, n_jobs=len(jobs)),
        grid=grid,
        in_specs=in_specs + job_in,
        out_specs=[pl.BlockSpec((bm, bn), lambda i, j: (i, j))] + job_out,
        out_shape=[jax.ShapeDtypeStruct((m, n), BF16)] + job_shape,
        compiler_params=_compiler_params(("arbitrary", "arbitrary")),
        name="matmul",
    )(*operands, *[w for w, _ in jobs])
    return outs[0], list(outs[1:])


def _mm_swiglu_kernel(*refs, normed, n_jobs):
    ins, job_src, (o_ref,), job_dst = _split_refs(refs, 4 if normed else 3, 1, n_jobs)
    x_ref, wg_ref, wu_ref = ins[0], ins[-2], ins[-1]
    x = x_ref[...]
    gate = jnp.dot(x, wg_ref[...], preferred_element_type=F32)
    up = jnp.dot(x, wu_ref[...], preferred_element_type=F32)
    if normed:
        r = _row_scale(ins[1], x_ref.shape[1])
        gate = gate * r
        up = up * r
    o_ref[...] = (gate * jax.nn.sigmoid(gate) * up).astype(o_ref.dtype)
    _run_cast_jobs(job_src, job_dst)


def matmul_swiglu(x, ssq, w_in, *, jobs=(), bm=1024):
    m, k = x.shape
    f = w_in.shape[-1] // 2
    bm = _pick(m, bm)
    bn = _pick_bn(f, k, bm, 2, 2, lambda bn: _cast_job_bytes(jobs, (m // bm) * (f // bn)))
    nf = f // bn
    grid = (m // bm, nf)
    normed = ssq is not None
    operands = (x, ssq, w_in, w_in) if normed else (x, w_in, w_in)
    in_specs = [pl.BlockSpec((bm, k), lambda i, j: (i, 0))]
    if normed:
        in_specs.append(pl.BlockSpec((bm, SSQ_LANES), lambda i, j: (i, 0)))
    in_specs += [_weight_spec(w_in, (), k, bn), _weight_spec(w_in, (), k, bn, nf)]
    job_in, job_out, job_shape = _cast_job_specs(jobs, grid)
    outs = pl.pallas_call(
        functools.partial(_mm_swiglu_kernel, normed=normed, n_jobs=len(jobs)),
        grid=grid,
        in_specs=in_specs + job_in,
        out_specs=[pl.BlockSpec((bm, bn), lambda i, j: (i, j))] + job_out,
        out_shape=[jax.ShapeDtypeStruct((m, f), BF16)] + job_shape,
        compiler_params=_compiler_params(("arbitrary", "arbitrary")),
        name="matmul_swiglu",
    )(*operands, *[w for w, _ in jobs])
    return outs[0], list(outs[1:])


def _mm_residual_kernel(*refs, alpha, n_gains, n_jobs):
    ins, job_src, outs, job_dst = _split_refs(refs, 3 + n_gains, 1 + n_gains + bool(n_gains),
                                              n_jobs)
    x_ref, w_ref, h_ref = ins[:3]
    g_refs = ins[3:]
    o_ref = outs[0]
    og_refs = outs[1:1 + n_gains]
    acc = jnp.dot(x_ref[...], w_ref[...], preferred_element_type=F32)
    out = h_ref[...] + alpha * acc
    o_ref[...] = out
    _run_cast_jobs(job_src, job_dst)
    if n_gains:
        ssq_ref = outs[-1]
        for g_ref, og_ref in zip(g_refs, og_refs):
            og_ref[...] = (out * g_ref[...]).astype(og_ref.dtype)
        sq = out * out
        part = sq[:, :SSQ_LANES]
        for c in range(1, out.shape[1] // SSQ_LANES):
            part = part + sq[:, c * SSQ_LANES:(c + 1) * SSQ_LANES]

        @pl.when(pl.program_id(1) == 0)
        def _():
            ssq_ref[...] = part

        @pl.when(pl.program_id(1) != 0)
        def _():
            ssq_ref[...] += part


def matmul_residual(x, w, h, alpha, gains=(), *, jobs=(), bm=1024):
    m, k = x.shape
    n = w.shape[-1]
    bm = _pick(m, bm)
    bn = _pick_bn(n, k, bm, 1, 4 + 4 + 2 * len(gains),
                  lambda bn: _cast_job_bytes(jobs, (m // bm) * (n // bn)))
    grid = (m // bm, n // bn)
    tile = pl.BlockSpec((bm, bn), lambda i, j: (i, j))
    in_specs = [pl.BlockSpec((bm, k), lambda i, j: (i, 0)), _weight_spec(w, (), k, bn), tile]
    in_specs += [pl.BlockSpec((1, bn), lambda i, j: (0, j))] * len(gains)
    out_specs = [tile] * (1 + len(gains))
    out_shape = [jax.ShapeDtypeStruct((m, n), F32)]
    out_shape += [jax.ShapeDtypeStruct((m, n), BF16)] * len(gains)
    if gains:
        assert bn % SSQ_LANES == 0
        out_specs.append(pl.BlockSpec((bm, SSQ_LANES), lambda i, j: (i, 0)))
        out_shape.append(jax.ShapeDtypeStruct((m, SSQ_LANES), F32))
    n_main = len(out_specs)
    job_in, job_out, job_shape = _cast_job_specs(jobs, grid)
    outs = pl.pallas_call(
        functools.partial(_mm_residual_kernel, alpha=alpha, n_gains=len(gains),
                          n_jobs=len(jobs)),
        grid=grid,
        in_specs=in_specs + job_in,
        out_specs=out_specs + job_out,
        out_shape=out_shape + job_shape,
        compiler_params=_compiler_params(("arbitrary", "arbitrary")),
        name="matmul_residual",
    )(x, w, h, *[g.reshape(1, n) for g in gains], *[w for w, _ in jobs])
    ssq = outs[n_main - 1] if gains else None
    return outs[0], list(outs[1:1 + len(gains)]), ssq, list(outs[n_main:])


def _sb_attention_kernel(q_ref, k_ref, v_ref, o_ref, kt_ref, acc_ref,
                         *, seq, blk, group):
    nblk = seq // blk
    for g in range(group):
        lanes = slice(g * HEAD_DIM, (g + 1) * HEAD_DIM)
        for b in range(nblk):
            rows = slice(b * blk, (b + 1) * blk)
            kt_ref[g, b] = k_ref[rows, lanes].T

    row = lax.broadcasted_iota(jnp.int32, (blk, blk), 0)
    col = lax.broadcasted_iota(jnp.int32, (blk, blk), 1)
    lower = (row > col).astype(BF16)
    lower2 = jnp.concatenate([lower, lower], axis=0)
    strict = col < row

    def blocks(i, j, nkb, carries, masked):
        chains = [(g, b) for g in range(group) for b in range(nkb)]
        lanes = [slice(g * HEAD_DIM, (g + 1) * HEAD_DIM) for g in range(group)]
        q_rows = pl.ds(pl.multiple_of(i * blk, blk), blk)
        zs = [jnp.dot(q_ref[q_rows, lanes[g]], kt_ref[g, j - b],
                      preferred_element_type=F32) for g, b in chains]
        log_betas, splits, before = [], [], []
        carries = list(carries)
        for c, (g, b) in enumerate(chains):
            z = zs[c]
            neg_abs = lax.bitcast_convert_type(
                lax.bitcast_convert_type(z, jnp.uint32) | jnp.uint32(0x80000000), F32)
            log_beta = jnp.minimum(z, 0.0) - jnp.log(1.0 + jnp.exp2(neg_abs)) * LOG2_E
            log_1m = log_beta - z
            if masked and b == 0:
                log_1m = jnp.where(strict, log_1m, 0.0)
            hi = log_1m.astype(BF16)
            lo = (log_1m - hi.astype(F32)).astype(BF16)
            log_betas.append(log_beta)
            splits.append(jnp.concatenate([hi, lo], axis=1))
            before.append(carries[g])
            carries[g] = carries[g] + jnp.sum(log_1m, axis=1, keepdims=True)
        laters = [jnp.dot(split, lower2, preferred_element_type=F32) for split in splits]
        weights = []
        for c, (g, b) in enumerate(chains):
            a = jnp.exp2(log_betas[c] + (laters[c] + before[c]))
            if masked and b == 0:
                a = jnp.where(strict, a, 0.0)
            weights.append(a.astype(BF16))
        for g in range(group):
            pv = None
            for c, (cg, b) in enumerate(chains):
                if cg == g:
                    k_rows = pl.ds(pl.multiple_of((j - b) * blk, blk), blk)
                    term = jnp.dot(weights[c], v_ref[k_rows, lanes[g]],
                                   preferred_element_type=F32)
                    pv = term if pv is None else pv + term
            if masked:
                acc_ref[g] = pv
            else:
                acc_ref[g] += pv
        return tuple(carries)

    def any_live(carries):
        top = carries[0]
        for c in carries[1:]:
            top = jnp.maximum(top, c)
        return jnp.max(top) > EXP2_UNDERFLOW

    def write_out(i):
        out_rows = pl.ds(pl.multiple_of(i * blk, blk), blk)
        for g in range(group):
            o_ref[out_rows, g * HEAD_DIM:(g + 1) * HEAD_DIM] = acc_ref[g].astype(o_ref.dtype)

    zero = jnp.zeros((blk, 1), F32)
    blocks(0, 0, 1, (zero,) * group, True)
    write_out(0)

    def q_block(i, _):
        carries = blocks(i, i, 2, (zero,) * group, True)

        def cond(state):
            return jnp.logical_and(state[0] < i, state[1])

        def body(state):
            n = state[0]
            new = blocks(i, i - 1 - n, 1, state[2:], False)
            return (n + 1, any_live(new)) + new

        lax.while_loop(cond, body, (jnp.int32(1), any_live(carries)) + carries)
        write_out(i)
        return 0

    lax.fori_loop(1, nblk, q_block, 0)


def sb_attention(qkv, batch, seq):
    d = qkv.shape[1] // 3
    heads = d // HEAD_DIM
    group = math.gcd(heads, SB_HEADS_PER_STEP)
    steps = heads // group
    width = group * HEAD_DIM
    blk = _pick(seq, ATTN_BLOCK)
    nblk = seq // blk
    kernel = functools.partial(_sb_attention_kernel, seq=seq, blk=blk, group=group)
    return pl.pallas_call(
        kernel,
        grid=(batch, steps),
        in_specs=[pl.BlockSpec((seq, width), lambda b, h: (b, h)),
                  pl.BlockSpec((seq, width), lambda b, h: (b, steps + h)),
                  pl.BlockSpec((seq, width), lambda b, h: (b, 2 * steps + h))],
        out_specs=pl.BlockSpec((seq, width), lambda b, h: (b, h)),
        out_shape=jax.ShapeDtypeStruct((batch * seq, d), BF16),
        scratch_shapes=[pltpu.VMEM((group, nblk, HEAD_DIM, blk), BF16),
                        pltpu.VMEM((group, blk, HEAD_DIM), F32)],
        compiler_params=_compiler_params(("arbitrary", "arbitrary")),
        name="sb_attention",
    )(qkv, qkv, qkv)


def _diff_attention_kernel(lq1_ref, lk1_ref, lq2_ref, lk2_ref, g_ref, q_ref, k_ref, v_ref,
                           o_ref, qt_ref, vt_ref, acc_ref, bias_ref,
                           *, seq, blk, group, heads, lam_init):
    nblk = seq // blk
    dv = 2 * HEAD_DIM
    for g in range(group):
        for b in range(nblk):
            rows = slice(b * blk, (b + 1) * blk)
            for c in range(2):
                lanes = slice(g * dv + c * HEAD_DIM, g * dv + (c + 1) * HEAD_DIM)
                qt_ref[2 * g + c, b] = q_ref[rows, lanes].T
            vt_ref[g, b] = v_ref[rows, g * dv:(g + 1) * dv].T

    s1 = jnp.sum(lq1_ref[...] * lk1_ref[...], axis=-1, keepdims=True)
    s2 = jnp.sum(lq2_ref[...] * lk2_ref[...], axis=-1, keepdims=True)
    lam = jnp.exp(s1) - jnp.exp(s2) + lam_init
    out_gain = g_ref[...] * (1.0 - lam_init)

    first_head = jnp.full((1, 1), pl.program_id(1) * group, jnp.int32).astype(F32)
    slopes = [jnp.exp2(-8.0 * (first_head + (g + 1.0)) / heads) * LOG2_E
              for g in range(group)]

    row = lax.broadcasted_iota(jnp.int32, (blk, blk), 0)
    col = lax.broadcasted_iota(jnp.int32, (blk, blk), 1)
    dist = (col - row).astype(F32)
    allowed = (row // STREAM_CHUNK) <= (col // STREAM_CHUNK)
    for g in range(group):
        bias_ref[2 * g] = slopes[g] * dist
        bias_ref[2 * g + 1] = jnp.where(allowed, slopes[g] * jnp.abs(dist), jnp.inf)

    nset = 2 * group

    def update(i, j, nkb, state, masked):
        sets = range(nset)
        kbs = range(nkb)
        zs = []
        for u in sets:
            g, c = divmod(u, 2)
            lanes = slice(g * dv + c * HEAD_DIM, g * dv + (c + 1) * HEAD_DIM)
            for b in kbs:
                k_rows = pl.ds(pl.multiple_of((j + b) * blk, blk), blk)
                zs.append(jnp.dot(k_ref[k_rows, lanes], qt_ref[u, i],
                                  preferred_element_type=F32))
        new_state, alphas, ps = (), [], []
        for u in sets:
            m, l = state[2 * u], state[2 * u + 1]
            if masked:
                ys = [zs[u] - bias_ref[2 * (u // 2) + 1]]
                fars = [0.0]
            else:
                fars = [slopes[u // 2] * jnp.asarray((i - j - b) * blk, F32) for b in kbs]
                ys = [zs[u * nkb + b] - bias_ref[2 * (u // 2)] for b in kbs]
            m_new = m
            for y, far in zip(ys, fars):
                m_new = jnp.maximum(m_new, jnp.max(y, axis=0, keepdims=True) - far)
            alpha = jnp.exp2(m - m_new)
            l = l * alpha
            for y, far in zip(ys, fars):
                p = jnp.exp2(y - (m_new + far))
                l = l + jnp.sum(p, axis=0, keepdims=True)
                ps.append(p.astype(BF16))
            new_state += (m_new, l)
            alphas.append(alpha)
        for u in sets:
            pv = jnp.dot(vt_ref[u // 2, j], ps[u * nkb], preferred_element_type=F32)
            for b in range(1, nkb):
                pv += jnp.dot(vt_ref[u // 2, j + b], ps[u * nkb + b],
                              preferred_element_type=F32)
            if masked:
                acc_ref[u] = pv
            else:
                acc_ref[u] = acc_ref[u] * alphas[u] + pv
        return new_state

    def q_block(i, _):
        neg = jnp.full((1, blk), -jnp.inf, F32)
        zero = jnp.zeros((1, blk), F32)
        state = update(i, i, 1, (neg, zero) * nset, True)

        def pair(n, st):
            return update(i, i - 2 - 2 * n, 2, st, False)

        def single(n, st):
            return update(i, 0, 1, st, False)

        state = lax.fori_loop(0, i // 2, pair, state)
        state = lax.fori_loop(0, i % 2, single, state)
        out_rows = pl.ds(pl.multiple_of(i * blk, blk), blk)
        for g in range(group):
            inv1, inv2 = 1.0 / state[4 * g + 1], lam / state[4 * g + 3]
            ot = acc_ref[2 * g] * inv1 - acc_ref[2 * g + 1] * inv2
            o = ot.T
            ms = jnp.mean(o * o, axis=-1, keepdims=True)
            y = o * lax.rsqrt(ms + RMS_EPS) * out_gain
            o_ref[out_rows, g * dv:(g + 1) * dv] = y.astype(o_ref.dtype)
        return 0

    lax.fori_loop(0, nblk, q_block, 0)


def diff_attention(q, kv, lq1, lk1, lq2, lk2, gain, batch, seq, lam_init):
    d = q.shape[1]
    dv = 2 * HEAD_DIM
    heads = d // dv
    group = math.gcd(heads, DIFF_HEADS_PER_STEP)
    steps = heads // group
    width = group * dv
    blk = _pick(seq, ATTN_BLOCK)
    nblk = seq // blk
    kernel = functools.partial(_diff_attention_kernel, seq=seq, blk=blk, group=group,
                               heads=heads, lam_init=lam_init)
    vec = pl.BlockSpec((1, HEAD_DIM), lambda b, h: (0, 0))
    head_blk = pl.BlockSpec((seq, width), lambda b, h: (b, h))
    return pl.pallas_call(
        kernel,
        grid=(batch, steps),
        in_specs=[vec, vec, vec, vec,
                  pl.BlockSpec((1, dv), lambda b, h: (0, 0)),
                  head_blk, head_blk,
                  pl.BlockSpec((seq, width), lambda b, h: (b, steps + h))],
        out_specs=head_blk,
        out_shape=jax.ShapeDtypeStruct((batch * seq, d), BF16),
        scratch_shapes=[pltpu.VMEM((2 * group, nblk, HEAD_DIM, blk), BF16),
                        pltpu.VMEM((group, nblk, dv, blk), BF16),
                        pltpu.VMEM((2 * group, dv, blk), F32),
                        pltpu.VMEM((2 * group, blk, blk), F32)],
        compiler_params=_compiler_params(("arbitrary", "arbitrary")),
        name="diff_attention",
    )(lq1.reshape(1, -1), lk1.reshape(1, -1), lq2.reshape(1, -1), lk2.reshape(1, -1),
      gain.reshape(1, -1), q, kv, kv)


def kernel(x, ffn_norm, w_ffn_in, w_ffn_out, attn_norm, w_qkv_a, w_o_a, kv_norm, w_kv_b, w_q_b,
           lambda_q1, lambda_k1, lambda_q2, lambda_k2, subln_norm, w_o_b, final_norm):
    batch, seq, d = x.shape
    depth = ffn_norm.shape[0]
    n_a = w_qkv_a.shape[0]
    h = x.reshape(batch * seq, d)
    j_of = lambda l: l - n_a

    ready = {}

    def job(name, *index):
        src = {"in": w_ffn_in, "out": w_ffn_out, "qkv": w_qkv_a, "oa": w_o_a, "kv": w_kv_b,
               "qb": w_q_b, "ob": w_o_b}[name]
        return (name,) + index, (src, index)

    def run(call, keyed_jobs):
        *main, cast = call(jobs=tuple(j for _, j in keyed_jobs))
        ready.update({key: w for (key, _), w in zip(keyed_jobs, cast)})
        return main

    ready[("in", 0, 0)] = w_ffn_in[0, 0].astype(BF16)
    xg, ssq = rmsnorm(h, ffn_norm[0, 0], BF16), None
    kv_shared = None
    for l in range(depth):
        sb_layer = l < n_a
        proj = job("qkv", l) if sb_layer else job("qb", j_of(l))
        out_proj = job("oa", l) if sb_layer else job("ob", j_of(l))
        (hid,) = run(functools.partial(matmul_swiglu, xg, ssq, ready[("in", l, 0)]),
                     [job("out", l, 0), proj])
        h, (xg,), ssq = matmul_residual(hid, ready[("out", l, 0)], h, 0.5, (attn_norm[l],))[:3]
        (qx,) = run(functools.partial(matmul, xg, ssq, ready[proj[0]], scaled_cols=d,
                                      factor=QK_LOG2_SCALE), [out_proj])
        if sb_layer:
            o = sb_attention(qx, batch, seq)
        else:
            lam_init = 0.8 - 0.6 * math.exp(-0.3 * l)
            jj = j_of(l)
            o = diff_attention(qx, kv_shared, lambda_q1[jj], lambda_k1[jj], lambda_q2[jj],
                               lambda_k2[jj], subln_norm[jj], batch, seq, lam_init)
        h, (xg,), ssq = run(functools.partial(matmul_residual, o, ready[out_proj[0]], h, 1.0,
                                              (ffn_norm[l, 1],)), [job("in", l, 1)])
        later = [job("out", l, 1)]
        if l + 1 < depth:
            later.append(job("in", l + 1, 0))
        if l == n_a - 1:
            later.append(job("kv"))
        (hid,) = run(functools.partial(matmul_swiglu, xg, ssq, ready[("in", l, 1)]), later)
        gains = (ffn_norm[l + 1, 0],) if l + 1 < depth else ()
        if l == n_a - 1:
            gains += (kv_norm,)
        h, normed, ssq = matmul_residual(hid, ready[("out", l, 1)], h, 0.5, gains)[:3]
        xg = normed[0] if l + 1 < depth else None
        if l == n_a - 1:
            kv_shared = matmul(normed[-1], ssq, ready[("kv",)])[0]
    return rmsnorm(h, final_norm, F32).reshape(batch, seq, d)
```

```python
import functools
import math

import jax
import jax.numpy as jnp
from jax import lax
from jax.experimental import pallas as pl
from jax.experimental.pallas import tpu as pltpu

F32 = jnp.float32
BF16 = jnp.bfloat16

HEAD_DIM = 128
STREAM_CHUNK = 64
RMS_EPS = 1e-6
ATTN_BLOCK = 256
SB_HEADS_PER_STEP = 4
DIFF_HEADS_PER_STEP = 4
LOG2_E = 1.4426950408889634
QK_LOG2_SCALE = LOG2_E / math.sqrt(HEAD_DIM)
EXP2_UNDERFLOW = -150.0
VMEM_LIMIT_BYTES = 56 * 1024 * 1024
VMEM_TILE_BUDGET = 51 * 1024 * 1024
SSQ_LANES = 128


def _compiler_params(semantics):
    return pltpu.CompilerParams(dimension_semantics=semantics,
                                vmem_limit_bytes=VMEM_LIMIT_BYTES)


def _pick(n, pref):
    if n <= pref:
        return n
    t = pref
    while n % t:
        t //= 2
    return t


def _pick_bn(n, k, bm, weight_tiles, tile_bytes, extra_bytes):
    for bn in (1024, 512, 256, 128):
        if n % bn:
            continue
        need = (2 * bm * k * 2 + weight_tiles * (2 * k * bn * 2 + bm * bn * 4)
                + 2 * bm * bn * tile_bytes + extra_bytes(bn))
        if need <= VMEM_TILE_BUDGET:
            return bn
    return n if n < 128 else 128


def _rmsnorm_kernel(x_ref, g_ref, o_ref):
    x = x_ref[...]
    ms = jnp.mean(x * x, axis=-1, keepdims=True)
    o_ref[...] = (x * lax.rsqrt(ms + RMS_EPS) * g_ref[...]).astype(o_ref.dtype)


def rmsnorm(x, g, out_dtype):
    m, d = x.shape
    bm = _pick(m, 512)
    return pl.pallas_call(
        _rmsnorm_kernel,
        grid=(m // bm,),
        in_specs=[pl.BlockSpec((bm, d), lambda i: (i, 0)),
                  pl.BlockSpec((1, d), lambda i: (0, 0))],
        out_specs=pl.BlockSpec((bm, d), lambda i: (i, 0)),
        out_shape=jax.ShapeDtypeStruct((m, d), out_dtype),
        compiler_params=_compiler_params(("arbitrary",)),
        name="rmsnorm",
    )(x, g.reshape(1, d))


def _row_scale(ssq_ref, width):
    return lax.rsqrt(jnp.sum(ssq_ref[...], axis=-1, keepdims=True) * (1.0 / width) + RMS_EPS)


def _weight_spec(k, bn, col_offset=0):
    return pl.BlockSpec((k, bn), lambda i, j: (0, col_offset + j))


def _cast_slab_rows(k, steps):
    return next(r for r in range(16, k + 1, 16) if k % r == 0 and k // r <= steps)


def _cast_job_bytes(jobs, steps):
    return sum(2 * _cast_slab_rows(w.shape[-2], steps) * w.shape[-1] * (4 + 2) for w, _ in jobs)


def _cast_job_specs(jobs, grid):
    steps = grid[0] * grid[1]
    in_specs, out_specs, out_shape = [], [], []
    for w, index in jobs:
        k, n = w.shape[-2:]
        rb = _cast_slab_rows(k, steps)
        lead = tuple(index)

        def slab(i, j, last=k // rb - 1):
            return jnp.minimum(i * grid[1] + j, last)

        in_specs.append(pl.BlockSpec((None,) * len(lead) + (rb, n),
                                     lambda i, j, lead=lead, slab=slab: lead + (slab(i, j), 0)))
        out_specs.append(pl.BlockSpec((rb, n), lambda i, j, slab=slab: (slab(i, j), 0)))
        out_shape.append(jax.ShapeDtypeStruct((k, n), BF16))
    return in_specs, out_specs, out_shape


def _run_cast_jobs(src_refs, dst_refs):
    for src, dst in zip(src_refs, dst_refs):
        dst[...] = src[...].astype(dst.dtype)


def _split_refs(refs, n_in, n_out, n_jobs):
    a, b, c = n_in, n_in + n_jobs, n_in + n_jobs + n_out
    return refs[:a], refs[a:b], refs[b:c], refs[c:]


def _mm_kernel(*refs, scaled_tiles, factor, normed, n_jobs):
    ins, job_src, (o_ref,), job_dst = _split_refs(refs, 3 if normed else 2, 1, n_jobs)
    x_ref, w_ref = ins[0], ins[-1]
    acc = jnp.dot(x_ref[...], w_ref[...], preferred_element_type=F32)
    if normed:
        acc = acc * _row_scale(ins[1], x_ref.shape[1])
    if scaled_tiles:
        acc = acc * jnp.where(pl.program_id(1) < scaled_tiles, factor, 1.0)
    o_ref[...] = acc.astype(o_ref.dtype)
    _run_cast_jobs(job_src, job_dst)


def matmul(x, ssq, w, *, scaled_cols=0, factor=1.0, jobs=(), bm=1024):
    m, k = x.shape
    n = w.shape[-1]
    bm = _pick(m, bm)
    bn = _pick_bn(n, k, bm, 1, 2, lambda bn: _cast_job_bytes(jobs, (m // bm) * (n // bn)))
    assert scaled_cols % bn == 0
    grid = (m // bm, n // bn)
    normed = ssq is not None
    operands = (x, ssq, w) if normed else (x, w)
    in_specs = [pl.BlockSpec((bm, k), lambda i, j: (i, 0))]
    if normed:
        in_specs.append(pl.BlockSpec((bm, SSQ_LANES), lambda i, j: (i, 0)))
    in_specs.append(_weight_spec(k, bn))
    job_in, job_out, job_shape = _cast_job_specs(jobs, grid)
    outs = pl.pallas_call(
        functools.partial(_mm_kernel, scaled_tiles=scaled_cols // bn, factor=factor,
                          normed=normed, n_jobs=len(jobs)),
        grid=grid,
        in_specs=in_specs + job_in,
        out_specs=[pl.BlockSpec((bm, bn), lambda i, j: (i, j))] + job_out,
        out_shape=[jax.ShapeDtypeStruct((m, n), BF16)] + job_shape,
        compiler_params=_compiler_params(("arbitrary", "arbitrary")),
        name="matmul",
    )(*operands, *[w for w, _ in jobs])
    return outs[0], list(outs[1:])


def _mm_swiglu_kernel(*refs, normed, n_jobs):
    ins, job_src, (o_ref,), job_dst = _split_refs(refs, 4 if normed else 3, 1, n_jobs)
    x_ref, wg_ref, wu_ref = ins[0], ins[-2], ins[-1]
    x = x_ref[...]
    gate = jnp.dot(x, wg_ref[...], preferred_element_type=F32)
    up = jnp.dot(x, wu_ref[...], preferred_element_type=F32)
    if normed:
        r = _row_scale(ins[1], x_ref.shape[1])
        gate = gate * r
        up = up * r
    o_ref[...] = (gate * jax.nn.sigmoid(gate) * up).astype(o_ref.dtype)
    _run_cast_jobs(job_src, job_dst)


def matmul_swiglu(x, ssq, w_in, *, jobs=(), bm=1024):
    m, k = x.shape
    f = w_in.shape[-1] // 2
    bm = _pick(m, bm)
    bn = _pick_bn(f, k, bm, 2, 2, lambda bn: _cast_job_bytes(jobs, (m // bm) * (f // bn)))
    nf = f // bn
    grid = (m // bm, nf)
    normed = ssq is not None
    operands = (x, ssq, w_in, w_in) if normed else (x, w_in, w_in)
    in_specs = [pl.BlockSpec((bm, k), lambda i, j: (i, 0))]
    if normed:
        in_specs.append(pl.BlockSpec((bm, SSQ_LANES), lambda i, j: (i, 0)))
    in_specs += [_weight_spec(k, bn), _weight_spec(k, bn, nf)]
    job_in, job_out, job_shape = _cast_job_specs(jobs, grid)
    outs = pl.pallas_call(
        functools.partial(_mm_swiglu_kernel, normed=normed, n_jobs=len(jobs)),
        grid=grid,
        in_specs=in_specs + job_in,
        out_specs=[pl.BlockSpec((bm, bn), lambda i, j: (i, j))] + job_out,
        out_shape=[jax.ShapeDtypeStruct((m, f), BF16)] + job_shape,
        compiler_params=_compiler_params(("arbitrary", "arbitrary")),
        name="matmul_swiglu",
    )(*operands, *[w for w, _ in jobs])
    return outs[0], list(outs[1:])


def _mm_residual_kernel(*refs, alpha, n_gains, n_jobs):
    ins, job_src, outs, job_dst = _split_refs(refs, 3 + n_gains, 1 + n_gains + bool(n_gains),
                                              n_jobs)
    x_ref, w_ref, h_ref = ins[:3]
    g_refs = ins[3:]
    o_ref = outs[0]
    og_refs = outs[1:1 + n_gains]
    if n_gains:
        ssq_ref = outs[-1]

        @pl.when(pl.program_id(1) == 0)
        def _():
            ssq_ref[...] = jnp.zeros_like(ssq_ref)

    acc = jnp.dot(x_ref[...], w_ref[...], preferred_element_type=F32)
    out = h_ref[...] + alpha * acc
    o_ref[...] = out
    _run_cast_jobs(job_src, job_dst)
    if n_gains:
        for g_ref, og_ref in zip(g_refs, og_refs):
            og_ref[...] = (out * g_ref[...]).astype(og_ref.dtype)
        sq = out * out
        part = sq[:, :SSQ_LANES]
        for c in range(1, out.shape[1] // SSQ_LANES):
            part = part + sq[:, c * SSQ_LANES:(c + 1) * SSQ_LANES]
        ssq_ref[...] += part


def matmul_residual(x, w, h, alpha, gains=(), *, jobs=(), bm=1024):
    m, k = x.shape
    n = w.shape[-1]
    bm = _pick(m, bm)
    bn = _pick_bn(n, k, bm, 1, 4 + 4 + 2 * len(gains),
                  lambda bn: _cast_job_bytes(jobs, (m // bm) * (n // bn)))
    grid = (m // bm, n // bn)
    tile = pl.BlockSpec((bm, bn), lambda i, j: (i, j))
    in_specs = [pl.BlockSpec((bm, k), lambda i, j: (i, 0)), _weight_spec(k, bn), tile]
    in_specs += [pl.BlockSpec((1, bn), lambda i, j: (0, j))] * len(gains)
    out_specs = [tile] * (1 + len(gains))
    out_shape = [jax.ShapeDtypeStruct((m, n), F32)]
    out_shape += [jax.ShapeDtypeStruct((m, n), BF16)] * len(gains)
    if gains:
        assert bn % SSQ_LANES == 0
        out_specs.append(pl.BlockSpec((bm, SSQ_LANES), lambda i, j: (i, 0)))
        out_shape.append(jax.ShapeDtypeStruct((m, SSQ_LANES), F32))
    n_main = len(out_specs)
    job_in, job_out, job_shape = _cast_job_specs(jobs, grid)
    outs = pl.pallas_call(
        functools.partial(_mm_residual_kernel, alpha=alpha, n_gains=len(gains),
                          n_jobs=len(jobs)),
        grid=grid,
        in_specs=in_specs + job_in,
        out_specs=out_specs + job_out,
        out_shape=out_shape + job_shape,
        compiler_params=_compiler_params(("arbitrary", "arbitrary")),
        name="matmul_residual",
    )(x, w, h, *[g.reshape(1, n) for g in gains], *[w for w, _ in jobs])
    ssq = outs[n_main - 1] if gains else None
    return outs[0], list(outs[1:1 + len(gains)]), ssq, list(outs[n_main:])


def _sb_attention_kernel(q_ref, k_ref, v_ref, o_ref, kt_ref, acc_ref,
                         *, seq, blk, group):
    nblk = seq // blk
    for g in range(group):
        lanes = slice(g * HEAD_DIM, (g + 1) * HEAD_DIM)
        for b in range(nblk):
            rows = slice(b * blk, (b + 1) * blk)
            kt_ref[g, b] = k_ref[rows, lanes].T

    row = lax.broadcasted_iota(jnp.int32, (blk, blk), 0)
    col = lax.broadcasted_iota(jnp.int32, (blk, blk), 1)
    lower = (row > col).astype(BF16)
    lower2 = jnp.concatenate([lower, lower], axis=0)
    strict = col < row

    def blocks(i, j, nkb, carries, masked):
        chains = [(g, b) for g in range(group) for b in range(nkb)]
        lanes = [slice(g * HEAD_DIM, (g + 1) * HEAD_DIM) for g in range(group)]
        q_rows = pl.ds(pl.multiple_of(i * blk, blk), blk)
        zs = [jnp.dot(q_ref[q_rows, lanes[g]], kt_ref[g, j - b],
                      preferred_element_type=F32) for g, b in chains]
        log_betas, splits, before = [], [], []
        carries = list(carries)
        for c, (g, b) in enumerate(chains):
            z = zs[c]
            neg_abs = lax.bitcast_convert_type(
                lax.bitcast_convert_type(z, jnp.uint32) | jnp.uint32(0x80000000), F32)
            log_beta = jnp.minimum(z, 0.0) - jnp.log(1.0 + jnp.exp2(neg_abs)) * LOG2_E
            log_1m = log_beta - z
            if masked and b == 0:
                log_1m = jnp.where(strict, log_1m, 0.0)
            hi = log_1m.astype(BF16)
            lo = (log_1m - hi.astype(F32)).astype(BF16)
            log_betas.append(log_beta)
            splits.append(jnp.concatenate([hi, lo], axis=1))
            before.append(carries[g])
            carries[g] = carries[g] + jnp.sum(log_1m, axis=1, keepdims=True)
        laters = [jnp.dot(split, lower2, preferred_element_type=F32) for split in splits]
        weights = []
        for c, (g, b) in enumerate(chains):
            a = jnp.exp2(log_betas[c] + (laters[c] + before[c]))
            if masked and b == 0:
                a = jnp.where(strict, a, 0.0)
            weights.append(a.astype(BF16))
        for g in range(group):
            pv = None
            for c, (cg, b) in enumerate(chains):
                if cg == g:
                    k_rows = pl.ds(pl.multiple_of((j - b) * blk, blk), blk)
                    term = jnp.dot(weights[c], v_ref[k_rows, lanes[g]],
                                   preferred_element_type=F32)
                    pv = term if pv is None else pv + term
            if masked:
                acc_ref[g] = pv
            else:
                acc_ref[g] += pv
        return tuple(carries)

    def any_live(carries):
        top = carries[0]
        for c in carries[1:]:
            top = jnp.maximum(top, c)
        return jnp.max(top) > EXP2_UNDERFLOW

    def write_out(i):
        out_rows = pl.ds(pl.multiple_of(i * blk, blk), blk)
        for g in range(group):
            o_ref[out_rows, g * HEAD_DIM:(g + 1) * HEAD_DIM] = acc_ref[g].astype(o_ref.dtype)

    zero = jnp.zeros((blk, 1), F32)
    blocks(0, 0, 1, (zero,) * group, True)
    write_out(0)

    def q_block(i, _):
        carries = blocks(i, i, 2, (zero,) * group, True)

        def cond(state):
            return jnp.logical_and(state[0] < i, state[1])

        def body(state):
            n = state[0]
            new = blocks(i, i - 1 - n, 1, state[2:], False)
            return (n + 1, any_live(new)) + new

        lax.while_loop(cond, body, (jnp.int32(1), any_live(carries)) + carries)
        write_out(i)
        return 0

    lax.fori_loop(1, nblk, q_block, 0)


def sb_attention(qkv, batch, seq):
    d = qkv.shape[1] // 3
    heads = d // HEAD_DIM
    group = math.gcd(heads, SB_HEADS_PER_STEP)
    steps = heads // group
    width = group * HEAD_DIM
    blk = _pick(seq, ATTN_BLOCK)
    nblk = seq // blk
    kernel = functools.partial(_sb_attention_kernel, seq=seq, blk=blk, group=group)
    return pl.pallas_call(
        kernel,
        grid=(batch, steps),
        in_specs=[pl.BlockSpec((seq, width), lambda b, h: (b, h)),
                  pl.BlockSpec((seq, width), lambda b, h: (b, steps + h)),
                  pl.BlockSpec((seq, width), lambda b, h: (b, 2 * steps + h))],
        out_specs=pl.BlockSpec((seq, width), lambda b, h: (b, h)),
        out_shape=jax.ShapeDtypeStruct((batch * seq, d), BF16),
        scratch_shapes=[pltpu.VMEM((group, nblk, HEAD_DIM, blk), BF16),
                        pltpu.VMEM((group, blk, HEAD_DIM), F32)],
        compiler_params=_compiler_params(("arbitrary", "arbitrary")),
        name="sb_attention",
    )(qkv, qkv, qkv)


def _diff_attention_kernel(lq1_ref, lk1_ref, lq2_ref, lk2_ref, g_ref, q_ref, k_ref, v_ref,
                           o_ref, qt_ref, vt_ref, acc_ref, bias_ref,
                           *, seq, blk, group, heads, lam_init):
    nblk = seq // blk
    dv = 2 * HEAD_DIM
    for g in range(group):
        for b in range(nblk):
            rows = slice(b * blk, (b + 1) * blk)
            for c in range(2):
                lanes = slice(g * dv + c * HEAD_DIM, g * dv + (c + 1) * HEAD_DIM)
                qt_ref[2 * g + c, b] = q_ref[rows, lanes].T
            vt_ref[g, b] = v_ref[rows, g * dv:(g + 1) * dv].T

    s1 = jnp.sum(lq1_ref[...] * lk1_ref[...], axis=-1, keepdims=True)
    s2 = jnp.sum(lq2_ref[...] * lk2_ref[...], axis=-1, keepdims=True)
    lam = jnp.exp(s1) - jnp.exp(s2) + lam_init
    out_gain = g_ref[...] * (1.0 - lam_init)

    first_head = jnp.full((1, 1), pl.program_id(1) * group, jnp.int32).astype(F32)
    slopes = [jnp.exp2(-8.0 * (first_head + (g + 1.0)) / heads) * LOG2_E
              for g in range(group)]

    row = lax.broadcasted_iota(jnp.int32, (blk, blk), 0)
    col = lax.broadcasted_iota(jnp.int32, (blk, blk), 1)
    dist = (col - row).astype(F32)
    allowed = (row // STREAM_CHUNK) <= (col // STREAM_CHUNK)
    for g in range(group):
        bias_ref[2 * g] = slopes[g] * dist
        bias_ref[2 * g + 1] = jnp.where(allowed, slopes[g] * jnp.abs(dist), jnp.inf)

    nset = 2 * group

    def update(i, j, nkb, slot, state, masked):
        sets = range(nset)
        kbs = range(nkb)
        zs = []
        for u in sets:
            g, c = divmod(u, 2)
            lanes = slice(g * dv + c * HEAD_DIM, g * dv + (c + 1) * HEAD_DIM)
            for b in kbs:
                k_rows = pl.ds(pl.multiple_of((j + b) * blk, blk), blk)
                zs.append(jnp.dot(k_ref[k_rows, lanes], qt_ref[u, i],
                                  preferred_element_type=F32))
        new_state, alphas, ps = (), [], []
        for u in sets:
            m, l = state[2 * u], state[2 * u + 1]
            if masked:
                ys = [zs[u] - bias_ref[2 * (u // 2) + 1]]
                fars = [0.0]
            else:
                fars = [slopes[u // 2] * jnp.asarray((i - j - b) * blk, F32) for b in kbs]
                ys = [zs[u * nkb + b] - bias_ref[2 * (u // 2)] for b in kbs]
            m_new = m
            for y, far in zip(ys, fars):
                m_new = jnp.maximum(m_new, jnp.max(y, axis=0, keepdims=True) - far)
            alpha = jnp.exp2(m - m_new)
            l = l * alpha
            for y, far in zip(ys, fars):
                p = jnp.exp2(y - (m_new + far))
                l = l + jnp.sum(p, axis=0, keepdims=True)
                ps.append(p.astype(BF16))
            new_state += (m_new, l)
            alphas.append(alpha)
        for u in sets:
            pv = jnp.dot(vt_ref[u // 2, j], ps[u * nkb], preferred_element_type=F32)
            for b in range(1, nkb):
                pv += jnp.dot(vt_ref[u // 2, j + b], ps[u * nkb + b],
                              preferred_element_type=F32)
            if masked:
                acc_ref[slot, u] = pv
            else:
                acc_ref[slot, u] = acc_ref[slot, u] * alphas[u] + pv
        return new_state

    def finalize(i, slot, state):
        out_rows = pl.ds(pl.multiple_of(i * blk, blk), blk)
        for g in range(group):
            inv1, inv2 = 1.0 / state[4 * g + 1], lam / state[4 * g + 3]
            ot = acc_ref[slot, 2 * g] * inv1 - acc_ref[slot, 2 * g + 1] * inv2
            o = ot.T
            ms = jnp.mean(o * o, axis=-1, keepdims=True)
            y = o * lax.rsqrt(ms + RMS_EPS) * out_gain
            o_ref[out_rows, g * dv:(g + 1) * dv] = y.astype(o_ref.dtype)

    neg = jnp.full((1, blk), -jnp.inf, F32)
    zero = jnp.zeros((1, blk), F32)
    fresh = (neg, zero) * nset

    def q_block(i, prev):
        slot = i % 2
        finalize(i - 1, 1 - slot, prev)
        state = update(i, i, 1, slot, fresh, True)

        def pair(n, st):
            return update(i, i - 2 - 2 * n, 2, slot, st, False)

        def single(n, st):
            return update(i, 0, 1, slot, st, False)

        state = lax.fori_loop(0, i // 2, pair, state)
        return lax.fori_loop(0, i % 2, single, state)

    first = update(0, 0, 1, 0, fresh, True)
    last = lax.fori_loop(1, nblk, q_block, first)
    finalize(nblk - 1, (nblk - 1) % 2, last)


def diff_attention(q, kv, lq1, lk1, lq2, lk2, gain, batch, seq, lam_init):
    d = q.shape[1]
    dv = 2 * HEAD_DIM
    heads = d // dv
    group = math.gcd(heads, DIFF_HEADS_PER_STEP)
    steps = heads // group
    width = group * dv
    blk = _pick(seq, ATTN_BLOCK)
    nblk = seq // blk
    kernel = functools.partial(_diff_attention_kernel, seq=seq, blk=blk, group=group,
                               heads=heads, lam_init=lam_init)
    vec = pl.BlockSpec((1, HEAD_DIM), lambda b, h: (0, 0))
    head_blk = pl.BlockSpec((seq, width), lambda b, h: (b, h))
    return pl.pallas_call(
        kernel,
        grid=(batch, steps),
        in_specs=[vec, vec, vec, vec,
                  pl.BlockSpec((1, dv), lambda b, h: (0, 0)),
                  head_blk, head_blk,
                  pl.BlockSpec((seq, width), lambda b, h: (b, steps + h))],
        out_specs=head_blk,
        out_shape=jax.ShapeDtypeStruct((batch * seq, d), BF16),
        scratch_shapes=[pltpu.VMEM((2 * group, nblk, HEAD_DIM, blk), BF16),
                        pltpu.VMEM((group, nblk, dv, blk), BF16),
                        pltpu.VMEM((2, 2 * group, dv, blk), F32),
                        pltpu.VMEM((2 * group, blk, blk), F32)],
        compiler_params=_compiler_params(("arbitrary", "arbitrary")),
        name="diff_attention",
    )(lq1.reshape(1, -1), lk1.reshape(1, -1), lq2.reshape(1, -1), lk2.reshape(1, -1),
      gain.reshape(1, -1), q, kv, kv)


def kernel(x, ffn_norm, w_ffn_in, w_ffn_out, attn_norm, w_qkv_a, w_o_a, kv_norm, w_kv_b, w_q_b,
           lambda_q1, lambda_k1, lambda_q2, lambda_k2, subln_norm, w_o_b, final_norm):
    batch, seq, d = x.shape
    depth = ffn_norm.shape[0]
    n_a = w_qkv_a.shape[0]
    h = x.reshape(batch * seq, d)
    j_of = lambda l: l - n_a

    ready = {}

    def job(name, *index):
        src = {"in": w_ffn_in, "out": w_ffn_out, "qkv": w_qkv_a, "oa": w_o_a, "kv": w_kv_b,
               "qb": w_q_b, "ob": w_o_b}[name]
        return (name,) + index, (src, index)

    def run(call, keyed_jobs):
        *main, cast = call(jobs=tuple(j for _, j in keyed_jobs))
        ready.update({key: w for (key, _), w in zip(keyed_jobs, cast)})
        return main

    ready[("in", 0, 0)] = w_ffn_in[0, 0].astype(BF16)
    xg, ssq = rmsnorm(h, ffn_norm[0, 0], BF16), None
    kv_shared = None
    for l in range(depth):
        sb_layer = l < n_a
        proj = job("qkv", l) if sb_layer else job("qb", j_of(l))
        out_proj = job("oa", l) if sb_layer else job("ob", j_of(l))
        (hid,) = run(functools.partial(matmul_swiglu, xg, ssq, ready[("in", l, 0)]),
                     [job("out", l, 0), proj])
        h, (xg,), ssq = matmul_residual(hid, ready[("out", l, 0)], h, 0.5, (attn_norm[l],))[:3]
        (qx,) = run(functools.partial(matmul, xg, ssq, ready[proj[0]], scaled_cols=d,
                                      factor=QK_LOG2_SCALE), [out_proj])
        if sb_layer:
            o = sb_attention(qx, batch, seq)
        else:
            lam_init = 0.8 - 0.6 * math.exp(-0.3 * l)
            jj = j_of(l)
            o = diff_attention(qx, kv_shared, lambda_q1[jj], lambda_k1[jj], lambda_q2[jj],
                               lambda_k2[jj], subln_norm[jj], batch, seq, lam_init)
        h, (xg,), ssq = run(functools.partial(matmul_residual, o, ready[out_proj[0]], h, 1.0,
                                              (ffn_norm[l, 1],)), [job("in", l, 1)])
        later = [job("out", l, 1)]
        if l + 1 < depth:
            later.append(job("in", l + 1, 0))
        if l == n_a - 1:
            later.append(job("kv"))
        (hid,) = run(functools.partial(matmul_swiglu, xg, ssq, ready[("in", l, 1)]), later)
        gains = (ffn_norm[l + 1, 0],) if l + 1 < depth else ()
        if l == n_a - 1:
            gains += (kv_norm,)
        h, normed, ssq = matmul_residual(hid, ready[("out", l, 1)], h, 0.5, gains)[:3]
        xg = normed[0] if l + 1 < depth else None
        if l == n_a - 1:
            kv_shared = matmul(normed[-1], ssq, ready[("kv",)])[0]
    return rmsnorm(h, final_norm, F32).reshape(batch, seq, d)
```

```python
import functools
import math

import jax
import jax.numpy as jnp
from jax import lax
from jax.experimental import pallas as pl
from jax.experimental.pallas import tpu as pltpu

F32 = jnp.float32
BF16 = jnp.bfloat16

HEAD_DIM = 128
STREAM_CHUNK = 64
RMS_EPS = 1e-6
ATTN_BLOCK = 256
SB_HEADS_PER_STEP = 4
DIFF_HEADS_PER_STEP = 4
LOG2_E = 1.4426950408889634
QK_LOG2_SCALE = LOG2_E / math.sqrt(HEAD_DIM)
EXP2_UNDERFLOW = -150.0
VMEM_LIMIT_BYTES = 56 * 1024 * 1024
VMEM_TILE_BUDGET = 51 * 1024 * 1024
SSQ_LANES = 128


def _compiler_params(semantics):
    return pltpu.CompilerParams(dimension_semantics=semantics,
                                vmem_limit_bytes=VMEM_LIMIT_BYTES)


def _pick(n, pref):
    if n <= pref:
        return n
    t = pref
    while n % t:
        t //= 2
    return t


def _pick_bn(n, k, bm, weight_tiles, tile_bytes, extra_bytes):
    for bn in (1024, 512, 256, 128):
        if n % bn:
            continue
        need = (2 * bm * k * 2 + weight_tiles * (2 * k * bn * 2 + bm * bn * 4)
                + 2 * bm * bn * tile_bytes + extra_bytes(bn))
        if need <= VMEM_TILE_BUDGET:
            return bn
    return n if n < 128 else 128


def _rmsnorm_kernel(x_ref, g_ref, o_ref):
    x = x_ref[...]
    ms = jnp.mean(x * x, axis=-1, keepdims=True)
    o_ref[...] = (x * lax.rsqrt(ms + RMS_EPS) * g_ref[...]).astype(o_ref.dtype)


def rmsnorm(x, g, out_dtype):
    m, d = x.shape
    bm = _pick(m, 512)
    return pl.pallas_call(
        _rmsnorm_kernel,
        grid=(m // bm,),
        in_specs=[pl.BlockSpec((bm, d), lambda i: (i, 0)),
                  pl.BlockSpec((1, d), lambda i: (0, 0))],
        out_specs=pl.BlockSpec((bm, d), lambda i: (i, 0)),
        out_shape=jax.ShapeDtypeStruct((m, d), out_dtype),
        compiler_params=_compiler_params(("arbitrary",)),
        name="rmsnorm",
    )(x, g.reshape(1, d))


def _row_scale(ssq_ref, width):
    return lax.rsqrt(jnp.sum(ssq_ref[...], axis=-1, keepdims=True) * (1.0 / width) + RMS_EPS)


def _weight_spec(k, bn, col_offset=0):
    return pl.BlockSpec((k, bn), lambda i, j: (0, col_offset + j))


def _cast_slab_rows(k, steps):
    return next(r for r in range(16, k + 1, 16) if k % r == 0 and k // r <= steps)


def _cast_job_bytes(jobs, steps):
    return sum(2 * _cast_slab_rows(w.shape[-2], steps) * w.shape[-1] * (4 + 2) for w, _ in jobs)


def _cast_job_specs(jobs, grid):
    steps = grid[0] * grid[1]
    in_specs, out_specs, out_shape = [], [], []
    for w, index in jobs:
        k, n = w.shape[-2:]
        rb = _cast_slab_rows(k, steps)
        lead = tuple(index)

        def slab(i, j, last=k // rb - 1):
            return jnp.minimum(i * grid[1] + j, last)

        in_specs.append(pl.BlockSpec((None,) * len(lead) + (rb, n),
                                     lambda i, j, lead=lead, slab=slab: lead + (slab(i, j), 0)))
        out_specs.append(pl.BlockSpec((rb, n), lambda i, j, slab=slab: (slab(i, j), 0)))
        out_shape.append(jax.ShapeDtypeStruct((k, n), BF16))
    return in_specs, out_specs, out_shape


def _run_cast_jobs(src_refs, dst_refs):
    for src, dst in zip(src_refs, dst_refs):
        dst[...] = src[...].astype(dst.dtype)


def _split_refs(refs, n_in, n_out, n_jobs):
    a, b, c = n_in, n_in + n_jobs, n_in + n_jobs + n_out
    return refs[:a], refs[a:b], refs[b:c], refs[c:]


def _mm_kernel(*refs, scaled_tiles, factor, normed, n_jobs):
    ins, job_src, (o_ref,), job_dst = _split_refs(refs, 3 if normed else 2, 1, n_jobs)
    x_ref, w_ref = ins[0], ins[-1]
    acc = jnp.dot(x_ref[...], w_ref[...], preferred_element_type=F32)
    if normed:
        acc = acc * _row_scale(ins[1], x_ref.shape[1])
    if scaled_tiles:
        acc = acc * jnp.where(pl.program_id(1) < scaled_tiles, factor, 1.0)
    o_ref[...] = acc.astype(o_ref.dtype)
    _run_cast_jobs(job_src, job_dst)


def matmul(x, ssq, w, *, scaled_cols=0, factor=1.0, jobs=(), bm=1024):
    m, k = x.shape
    n = w.shape[-1]
    bm = _pick(m, bm)
    bn = _pick_bn(n, k, bm, 1, 2, lambda bn: _cast_job_bytes(jobs, (m // bm) * (n // bn)))
    assert scaled_cols % bn == 0
    grid = (m // bm, n // bn)
    normed = ssq is not None
    operands = (x, ssq, w) if normed else (x, w)
    in_specs = [pl.BlockSpec((bm, k), lambda i, j: (i, 0))]
    if normed:
        in_specs.append(pl.BlockSpec((bm, SSQ_LANES), lambda i, j: (i, 0)))
    in_specs.append(_weight_spec(k, bn))
    job_in, job_out, job_shape = _cast_job_specs(jobs, grid)
    outs = pl.pallas_call(
        functools.partial(_mm_kernel, scaled_tiles=scaled_cols // bn, factor=factor,
                          normed=normed, n_jobs=len(jobs)),
        grid=grid,
        in_specs=in_specs + job_in,
        out_specs=[pl.BlockSpec((bm, bn), lambda i, j: (i, j))] + job_out,
        out_shape=[jax.ShapeDtypeStruct((m, n), BF16)] + job_shape,
        compiler_params=_compiler_params(("arbitrary", "arbitrary")),
        name="matmul",
    )(*operands, *[w for w, _ in jobs])
    return outs[0], list(outs[1:])


def _mm_swiglu_kernel(*refs, normed, n_jobs):
    ins, job_src, (o_ref,), job_dst = _split_refs(refs, 4 if normed else 3, 1, n_jobs)
    x_ref, wg_ref, wu_ref = ins[0], ins[-2], ins[-1]
    x = x_ref[...]
    gate = jnp.dot(x, wg_ref[...], preferred_element_type=F32)
    up = jnp.dot(x, wu_ref[...], preferred_element_type=F32)
    if normed:
        r = _row_scale(ins[1], x_ref.shape[1])
        gate = gate * r
        up = up * r
    o_ref[...] = (gate * jax.nn.sigmoid(gate) * up).astype(o_ref.dtype)
    _run_cast_jobs(job_src, job_dst)


def matmul_swiglu(x, ssq, w_in, *, jobs=(), bm=1024):
    m, k = x.shape
    f = w_in.shape[-1] // 2
    bm = _pick(m, bm)
    bn = _pick_bn(f, k, bm, 2, 2, lambda bn: _cast_job_bytes(jobs, (m // bm) * (f // bn)))
    nf = f // bn
    grid = (m // bm, nf)
    normed = ssq is not None
    operands = (x, ssq, w_in, w_in) if normed else (x, w_in, w_in)
    in_specs = [pl.BlockSpec((bm, k), lambda i, j: (i, 0))]
    if normed:
        in_specs.append(pl.BlockSpec((bm, SSQ_LANES), lambda i, j: (i, 0)))
    in_specs += [_weight_spec(k, bn), _weight_spec(k, bn, nf)]
    job_in, job_out, job_shape = _cast_job_specs(jobs, grid)
    outs = pl.pallas_call(
        functools.partial(_mm_swiglu_kernel, normed=normed, n_jobs=len(jobs)),
        grid=grid,
        in_specs=in_specs + job_in,
        out_specs=[pl.BlockSpec((bm, bn), lambda i, j: (i, j))] + job_out,
        out_shape=[jax.ShapeDtypeStruct((m, f), BF16)] + job_shape,
        compiler_params=_compiler_params(("arbitrary", "arbitrary")),
        name="matmul_swiglu",
    )(*operands, *[w for w, _ in jobs])
    return outs[0], list(outs[1:])


def _mm_residual_kernel(*refs, alpha, n_gains, n_jobs):
    ins, job_src, outs, job_dst = _split_refs(refs, 3 + n_gains, 1 + n_gains + bool(n_gains),
                                              n_jobs)
    x_ref, w_ref, h_ref = ins[:3]
    g_refs = ins[3:]
    o_ref = outs[0]
    og_refs = outs[1:1 + n_gains]
    if n_gains:
        ssq_ref = outs[-1]

        @pl.when(pl.program_id(1) == 0)
        def _():
            ssq_ref[...] = jnp.zeros_like(ssq_ref)

    acc = jnp.dot(x_ref[...], w_ref[...], preferred_element_type=F32)
    out = h_ref[...] + (acc if alpha == 1.0 else alpha * acc)
    o_ref[...] = out
    _run_cast_jobs(job_src, job_dst)
    if n_gains:
        for g_ref, og_ref in zip(g_refs, og_refs):
            og_ref[...] = (out * g_ref[...]).astype(og_ref.dtype)
        sq = out * out
        part = sq[:, :SSQ_LANES]
        for c in range(1, out.shape[1] // SSQ_LANES):
            part = part + sq[:, c * SSQ_LANES:(c + 1) * SSQ_LANES]
        ssq_ref[...] += part


def matmul_residual(x, w, h, alpha, gains=(), *, jobs=(), bm=1024):
    m, k = x.shape
    n = w.shape[-1]
    bm = _pick(m, bm)
    bn = _pick_bn(n, k, bm, 1, 4 + 4 + 2 * len(gains),
                  lambda bn: _cast_job_bytes(jobs, (m // bm) * (n // bn)))
    grid = (m // bm, n // bn)
    tile = pl.BlockSpec((bm, bn), lambda i, j: (i, j))
    in_specs = [pl.BlockSpec((bm, k), lambda i, j: (i, 0)), _weight_spec(k, bn), tile]
    in_specs += [pl.BlockSpec((1, bn), lambda i, j: (0, j))] * len(gains)
    out_specs = [tile] * (1 + len(gains))
    out_shape = [jax.ShapeDtypeStruct((m, n), F32)]
    out_shape += [jax.ShapeDtypeStruct((m, n), BF16)] * len(gains)
    if gains:
        assert bn % SSQ_LANES == 0
        out_specs.append(pl.BlockSpec((bm, SSQ_LANES), lambda i, j: (i, 0)))
        out_shape.append(jax.ShapeDtypeStruct((m, SSQ_LANES), F32))
    n_main = len(out_specs)
    job_in, job_out, job_shape = _cast_job_specs(jobs, grid)
    outs = pl.pallas_call(
        functools.partial(_mm_residual_kernel, alpha=alpha, n_gains=len(gains),
                          n_jobs=len(jobs)),
        grid=grid,
        in_specs=in_specs + job_in,
        out_specs=out_specs + job_out,
        out_shape=out_shape + job_shape,
        compiler_params=_compiler_params(("arbitrary", "arbitrary")),
        name="matmul_residual",
    )(x, w, h, *[g.reshape(1, n) for g in gains], *[w for w, _ in jobs])
    ssq = outs[n_main - 1] if gains else None
    return outs[0], list(outs[1:1 + len(gains)]), ssq, list(outs[n_main:])


def _sb_attention_kernel(q_ref, k_ref, v_ref, o_ref, kt_ref, acc_ref,
                         *, seq, blk, group):
    nblk = seq // blk
    for g in range(group):
        lanes = slice(g * HEAD_DIM, (g + 1) * HEAD_DIM)
        for b in range(nblk):
            rows = slice(b * blk, (b + 1) * blk)
            kt_ref[g, b] = k_ref[rows, lanes].T

    row = lax.broadcasted_iota(jnp.int32, (blk, blk), 0)
    col = lax.broadcasted_iota(jnp.int32, (blk, blk), 1)
    lower = (row > col).astype(BF16)
    lower2 = jnp.concatenate([lower, lower], axis=0)
    strict = col < row

    def blocks(i, j, nkb, carries, masked):
        chains = [(g, b) for g in range(group) for b in range(nkb)]
        lanes = [slice(g * HEAD_DIM, (g + 1) * HEAD_DIM) for g in range(group)]
        q_rows = pl.ds(pl.multiple_of(i * blk, blk), blk)
        zs = [jnp.dot(q_ref[q_rows, lanes[g]], kt_ref[g, j - b],
                      preferred_element_type=F32) for g, b in chains]
        log_betas, splits, before = [], [], []
        carries = list(carries)
        for c, (g, b) in enumerate(chains):
            z = zs[c]
            neg_abs = lax.bitcast_convert_type(
                lax.bitcast_convert_type(z, jnp.uint32) | jnp.uint32(0x80000000), F32)
            log_beta = jnp.minimum(z, 0.0) - jnp.log(1.0 + jnp.exp2(neg_abs)) * LOG2_E
            log_1m = log_beta - z
            if masked and b == 0:
                log_1m = jnp.where(strict, log_1m, 0.0)
            hi = log_1m.astype(BF16)
            lo = (log_1m - hi.astype(F32)).astype(BF16)
            log_betas.append(log_beta)
            splits.append(jnp.concatenate([hi, lo], axis=1))
            before.append(carries[g])
            carries[g] = carries[g] + jnp.sum(log_1m, axis=1, keepdims=True)
        laters = [jnp.dot(split, lower2, preferred_element_type=F32) for split in splits]
        weights = []
        for c, (g, b) in enumerate(chains):
            a = jnp.exp2(log_betas[c] + (laters[c] + before[c]))
            if masked and b == 0:
                a = jnp.where(strict, a, 0.0)
            weights.append(a.astype(BF16))
        for g in range(group):
            pv = None
            for c, (cg, b) in enumerate(chains):
                if cg == g:
                    k_rows = pl.ds(pl.multiple_of((j - b) * blk, blk), blk)
                    term = jnp.dot(weights[c], v_ref[k_rows, lanes[g]],
                                   preferred_element_type=F32)
                    pv = term if pv is None else pv + term
            if masked:
                acc_ref[g] = pv
            else:
                acc_ref[g] += pv
        return tuple(carries)

    def any_live(carries):
        top = carries[0]
        for c in carries[1:]:
            top = jnp.maximum(top, c)
        return jnp.max(top) > EXP2_UNDERFLOW

    def write_out(i):
        out_rows = pl.ds(pl.multiple_of(i * blk, blk), blk)
        for g in range(group):
            o_ref[out_rows, g * HEAD_DIM:(g + 1) * HEAD_DIM] = acc_ref[g].astype(o_ref.dtype)

    zero = jnp.zeros((blk, 1), F32)
    blocks(0, 0, 1, (zero,) * group, True)
    write_out(0)

    def q_block(i, _):
        carries = blocks(i, i, 2, (zero,) * group, True)

        def cond(state):
            return jnp.logical_and(state[0] < i, state[1])

        def body(state):
            n = state[0]
            new = blocks(i, i - 1 - n, 1, state[2:], False)
            return (n + 1, any_live(new)) + new

        lax.while_loop(cond, body, (jnp.int32(1), any_live(carries)) + carries)
        write_out(i)
        return 0

    lax.fori_loop(1, nblk, q_block, 0)


def sb_attention(qkv, batch, seq):
    d = qkv.shape[1] // 3
    heads = d // HEAD_DIM
    group = math.gcd(heads, SB_HEADS_PER_STEP)
    steps = heads // group
    width = group * HEAD_DIM
    blk = _pick(seq, ATTN_BLOCK)
    nblk = seq // blk
    kernel = functools.partial(_sb_attention_kernel, seq=seq, blk=blk, group=group)
    return pl.pallas_call(
        kernel,
        grid=(batch, steps),
        in_specs=[pl.BlockSpec((seq, width), lambda b, h: (b, h)),
                  pl.BlockSpec((seq, width), lambda b, h: (b, steps + h)),
                  pl.BlockSpec((seq, width), lambda b, h: (b, 2 * steps + h))],
        out_specs=pl.BlockSpec((seq, width), lambda b, h: (b, h)),
        out_shape=jax.ShapeDtypeStruct((batch * seq, d), BF16),
        scratch_shapes=[pltpu.VMEM((group, nblk, HEAD_DIM, blk), BF16),
                        pltpu.VMEM((group, blk, HEAD_DIM), F32)],
        compiler_params=_compiler_params(("arbitrary", "arbitrary")),
        name="sb_attention",
    )(qkv, qkv, qkv)


def _diff_attention_kernel(lq1_ref, lk1_ref, lq2_ref, lk2_ref, g_ref, q_ref, k_ref, v_ref,
                           o_ref, qt_ref, vt_ref, acc_ref, bias_ref,
                           *, seq, blk, group, heads, lam_init):
    nblk = seq // blk
    dv = 2 * HEAD_DIM
    for g in range(group):
        for b in range(nblk):
            rows = slice(b * blk, (b + 1) * blk)
            for c in range(2):
                lanes = slice(g * dv + c * HEAD_DIM, g * dv + (c + 1) * HEAD_DIM)
                qt_ref[2 * g + c, b] = q_ref[rows, lanes].T
            vt_ref[g, b] = v_ref[rows, g * dv:(g + 1) * dv].T

    s1 = jnp.sum(lq1_ref[...] * lk1_ref[...], axis=-1, keepdims=True)
    s2 = jnp.sum(lq2_ref[...] * lk2_ref[...], axis=-1, keepdims=True)
    lam = jnp.exp(s1) - jnp.exp(s2) + lam_init
    out_gain = g_ref[...] * (1.0 - lam_init)

    first_head = jnp.full((1, 1), pl.program_id(1) * group, jnp.int32).astype(F32)
    slopes = [jnp.exp2(-8.0 * (first_head + (g + 1.0)) / heads) * LOG2_E
              for g in range(group)]

    row = lax.broadcasted_iota(jnp.int32, (blk, blk), 0)
    col = lax.broadcasted_iota(jnp.int32, (blk, blk), 1)
    dist = (col - row).astype(F32)
    allowed = (row // STREAM_CHUNK) <= (col // STREAM_CHUNK)
    for g in range(group):
        bias_ref[2 * g] = slopes[g] * dist
        bias_ref[2 * g + 1] = jnp.where(allowed, slopes[g] * jnp.abs(dist), jnp.inf)

    nset = 2 * group

    def update(i, j, nkb, slot, state, masked):
        sets = range(nset)
        kbs = range(nkb)
        zs = []
        for u in sets:
            g, c = divmod(u, 2)
            lanes = slice(g * dv + c * HEAD_DIM, g * dv + (c + 1) * HEAD_DIM)
            for b in kbs:
                k_rows = pl.ds(pl.multiple_of((j + b) * blk, blk), blk)
                zs.append(jnp.dot(k_ref[k_rows, lanes], qt_ref[u, i],
                                  preferred_element_type=F32))
        new_state, alphas, ps = (), [], []
        for u in sets:
            m, l = state[2 * u], state[2 * u + 1]
            if masked:
                ys = [zs[u] - bias_ref[2 * (u // 2) + 1]]
                fars = [0.0]
            else:
                fars = [slopes[u // 2] * jnp.asarray((i - j - b) * blk, F32) for b in kbs]
                ys = [zs[u * nkb + b] - bias_ref[2 * (u // 2)] for b in kbs]
            m_new = m
            for y, far in zip(ys, fars):
                m_new = jnp.maximum(m_new, jnp.max(y, axis=0, keepdims=True) - far)
            alpha = jnp.exp2(m - m_new)
            l = l * alpha
            for y, far in zip(ys, fars):
                p = jnp.exp2(y - (m_new + far))
                l = l + jnp.sum(p, axis=0, keepdims=True)
                ps.append(p.astype(BF16))
            new_state += (m_new, l)
            alphas.append(alpha)
        for u in sets:
            pv = jnp.dot(vt_ref[u // 2, j], ps[u * nkb], preferred_element_type=F32)
            for b in range(1, nkb):
                pv += jnp.dot(vt_ref[u // 2, j + b], ps[u * nkb + b],
                              preferred_element_type=F32)
            if masked:
                acc_ref[slot, u] = pv
            else:
                acc_ref[slot, u] = acc_ref[slot, u] * alphas[u] + pv
        return new_state

    def finalize(i, slot, state):
        out_rows = pl.ds(pl.multiple_of(i * blk, blk), blk)
        for g in range(group):
            inv1, inv2 = 1.0 / state[4 * g + 1], lam / state[4 * g + 3]
            ot = acc_ref[slot, 2 * g] * inv1 - acc_ref[slot, 2 * g + 1] * inv2
            o = ot.T
            ms = jnp.mean(o * o, axis=-1, keepdims=True)
            y = o * lax.rsqrt(ms + RMS_EPS) * out_gain
            o_ref[out_rows, g * dv:(g + 1) * dv] = y.astype(o_ref.dtype)

    neg = jnp.full((1, blk), -jnp.inf, F32)
    zero = jnp.zeros((1, blk), F32)
    fresh = (neg, zero) * nset

    def q_block(i, prev):
        slot = i % 2
        finalize(i - 1, 1 - slot, prev)
        state = update(i, i, 1, slot, fresh, True)

        def pair(n, st):
            return update(i, i - 2 - 2 * n, 2, slot, st, False)

        def single(n, st):
            return update(i, 0, 1, slot, st, False)

        state = lax.fori_loop(0, i // 2, pair, state)
        return lax.fori_loop(0, i % 2, single, state)

    first = update(0, 0, 1, 0, fresh, True)
    last = lax.fori_loop(1, nblk, q_block, first)
    finalize(nblk - 1, (nblk - 1) % 2, last)


def diff_attention(q, kv, lq1, lk1, lq2, lk2, gain, batch, seq, lam_init):
    d = q.shape[1]
    dv = 2 * HEAD_DIM
    heads = d // dv
    group = math.gcd(heads, DIFF_HEADS_PER_STEP)
    steps = heads // group
    width = group * dv
    blk = _pick(seq, ATTN_BLOCK)
    nblk = seq // blk
    kernel = functools.partial(_diff_attention_kernel, seq=seq, blk=blk, group=group,
                               heads=heads, lam_init=lam_init)
    vec = pl.BlockSpec((1, HEAD_DIM), lambda b, h: (0, 0))
    head_blk = pl.BlockSpec((seq, width), lambda b, h: (b, h))
    return pl.pallas_call(
        kernel,
        grid=(batch, steps),
        in_specs=[vec, vec, vec, vec,
                  pl.BlockSpec((1, dv), lambda b, h: (0, 0)),
                  head_blk, head_blk,
                  pl.BlockSpec((seq, width), lambda b, h: (b, steps + h))],
        out_specs=head_blk,
        out_shape=jax.ShapeDtypeStruct((batch * seq, d), BF16),
        scratch_shapes=[pltpu.VMEM((2 * group, nblk, HEAD_DIM, blk), BF16),
                        pltpu.VMEM((group, nblk, dv, blk), BF16),
                        pltpu.VMEM((2, 2 * group, dv, blk), F32),
                        pltpu.VMEM((2 * group, blk, blk), F32)],
        compiler_params=_compiler_params(("arbitrary", "arbitrary")),
        name="diff_attention",
    )(lq1.reshape(1, -1), lk1.reshape(1, -1), lq2.reshape(1, -1), lk2.reshape(1, -1),
      gain.reshape(1, -1), q, kv, kv)


def kernel(x, ffn_norm, w_ffn_in, w_ffn_out, attn_norm, w_qkv_a, w_o_a, kv_norm, w_kv_b, w_q_b,
           lambda_q1, lambda_k1, lambda_q2, lambda_k2, subln_norm, w_o_b, final_norm):
    batch, seq, d = x.shape
    depth = ffn_norm.shape[0]
    n_a = w_qkv_a.shape[0]
    h = x.reshape(batch * seq, d)
    j_of = lambda l: l - n_a

    ready = {}

    def job(name, *index):
        src = {"in": w_ffn_in, "out": w_ffn_out, "qkv": w_qkv_a, "oa": w_o_a, "kv": w_kv_b,
               "qb": w_q_b, "ob": w_o_b}[name]
        return (name,) + index, (src, index)

    def run(call, keyed_jobs):
        *main, cast = call(jobs=tuple(j for _, j in keyed_jobs))
        ready.update({key: w for (key, _), w in zip(keyed_jobs, cast)})
        return main

    ready[("in", 0, 0)] = w_ffn_in[0, 0].astype(BF16)
    xg, ssq = rmsnorm(h, ffn_norm[0, 0], BF16), None
    kv_shared = None
    for l in range(depth):
        sb_layer = l < n_a
        proj = job("qkv", l) if sb_layer else job("qb", j_of(l))
        out_proj = job("oa", l) if sb_layer else job("ob", j_of(l))
        (hid,) = run(functools.partial(matmul_swiglu, xg, ssq, ready[("in", l, 0)]),
                     [job("out", l, 0), proj])
        h, (xg,), ssq = matmul_residual(hid, ready[("out", l, 0)], h, 0.5, (attn_norm[l],))[:3]
        (qx,) = run(functools.partial(matmul, xg, ssq, ready[proj[0]], scaled_cols=d,
                                      factor=QK_LOG2_SCALE), [out_proj, job("in", l, 1)])
        if sb_layer:
            o = sb_attention(qx, batch, seq)
        else:
            lam_init = 0.8 - 0.6 * math.exp(-0.3 * l)
            jj = j_of(l)
            o = diff_attention(qx, kv_shared, lambda_q1[jj], lambda_k1[jj], lambda_q2[jj],
                               lambda_k2[jj], subln_norm[jj], batch, seq, lam_init)
        h, (xg,), ssq = matmul_residual(o, ready[out_proj[0]], h, 1.0, (ffn_norm[l, 1],))[:3]
        later = [job("out", l, 1)]
        if l + 1 < depth:
            later.append(job("in", l + 1, 0))
        if l == n_a - 1:
            later.append(job("kv"))
        (hid,) = run(functools.partial(matmul_swiglu, xg, ssq, ready[("in", l, 1)]), later)
        gains = (ffn_norm[l + 1, 0],) if l + 1 < depth else ()
        if l == n_a - 1:
            gains += (kv_norm,)
        h, normed, ssq = matmul_residual(hid, ready[("out", l, 1)], h, 0.5, gains)[:3]
        xg = normed[0] if l + 1 < depth else None
        if l == n_a - 1:
            kv_shared = matmul(normed[-1], ssq, ready[("kv",)])[0]
    return rmsnorm(h, final_norm, F32).reshape(batch, seq, d)
```

```python
import functools
import math

import jax
import jax.numpy as jnp
from jax import lax
from jax.experimental import pallas as pl
from jax.experimental.pallas import tpu as pltpu

F32 = jnp.float32
BF16 = jnp.bfloat16

HEAD_DIM = 128
STREAM_CHUNK = 64
RMS_EPS = 1e-6
ATTN_BLOCK = 256
SB_HEADS_PER_STEP = 4
DIFF_HEADS_PER_STEP = 4
LOG2_E = 1.4426950408889634
QK_LOG2_SCALE = LOG2_E / math.sqrt(HEAD_DIM)
EXP2_UNDERFLOW = -150.0
VMEM_LIMIT_BYTES = 56 * 1024 * 1024
VMEM_TILE_BUDGET = 51 * 1024 * 1024
SSQ_LANES = 128


def _compiler_params(semantics):
    return pltpu.CompilerParams(dimension_semantics=semantics,
                                vmem_limit_bytes=VMEM_LIMIT_BYTES)


def _pick(n, pref):
    if n <= pref:
        return n
    t = pref
    while n % t:
        t //= 2
    return t


def _pick_bn(n, k, bm, weight_tiles, tile_bytes, extra_bytes):
    for bn in (1024, 512, 256, 128):
        if n % bn:
            continue
        need = (2 * bm * k * 2 + weight_tiles * (2 * k * bn * 2 + bm * bn * 4)
                + 2 * bm * bn * tile_bytes + extra_bytes(bn))
        if need <= VMEM_TILE_BUDGET:
            return bn
    return n if n < 128 else 128


def _rmsnorm_kernel(x_ref, g_ref, o_ref):
    x = x_ref[...]
    ms = jnp.mean(x * x, axis=-1, keepdims=True)
    o_ref[...] = (x * lax.rsqrt(ms + RMS_EPS) * g_ref[...]).astype(o_ref.dtype)


def rmsnorm(x, g, out_dtype):
    m, d = x.shape
    bm = _pick(m, 512)
    return pl.pallas_call(
        _rmsnorm_kernel,
        grid=(m // bm,),
        in_specs=[pl.BlockSpec((bm, d), lambda i: (i, 0)),
                  pl.BlockSpec((1, d), lambda i: (0, 0))],
        out_specs=pl.BlockSpec((bm, d), lambda i: (i, 0)),
        out_shape=jax.ShapeDtypeStruct((m, d), out_dtype),
        compiler_params=_compiler_params(("arbitrary",)),
        name="rmsnorm",
    )(x, g.reshape(1, d))


def _row_scale(ssq_ref, width):
    return lax.rsqrt(jnp.sum(ssq_ref[...], axis=-1, keepdims=True) * (1.0 / width) + RMS_EPS)


def _weight_spec(k, bn, col_offset=0):
    return pl.BlockSpec((k, bn), lambda i, j: (0, col_offset + j))


def _cast_slab_rows(k, steps):
    return next(r for r in range(16, k + 1, 16) if k % r == 0 and k // r <= steps)


def _cast_job_bytes(jobs, steps):
    return sum(2 * _cast_slab_rows(w.shape[-2], steps) * w.shape[-1] * (4 + 2) for w, _ in jobs)


def _cast_job_specs(jobs, grid):
    steps = grid[0] * grid[1]
    in_specs, out_specs, out_shape = [], [], []
    for w, index in jobs:
        k, n = w.shape[-2:]
        rb = _cast_slab_rows(k, steps)
        lead = tuple(index)

        def slab(i, j, last=k // rb - 1):
            return jnp.minimum(i * grid[1] + j, last)

        in_specs.append(pl.BlockSpec((None,) * len(lead) + (rb, n),
                                     lambda i, j, lead=lead, slab=slab: lead + (slab(i, j), 0)))
        out_specs.append(pl.BlockSpec((rb, n), lambda i, j, slab=slab: (slab(i, j), 0)))
        out_shape.append(jax.ShapeDtypeStruct((k, n), BF16))
    return in_specs, out_specs, out_shape


def _run_cast_jobs(src_refs, dst_refs):
    for src, dst in zip(src_refs, dst_refs):
        dst[...] = src[...].astype(dst.dtype)


def _split_refs(refs, n_in, n_out, n_jobs):
    a, b, c = n_in, n_in + n_jobs, n_in + n_jobs + n_out
    return refs[:a], refs[a:b], refs[b:c], refs[c:]


def _mm_kernel(*refs, scaled_tiles, factor, normed, n_jobs):
    ins, job_src, (o_ref,), job_dst = _split_refs(refs, 3 if normed else 2, 1, n_jobs)
    x_ref, w_ref = ins[0], ins[-1]
    acc = jnp.dot(x_ref[...], w_ref[...], preferred_element_type=F32)
    if normed:
        acc = acc * _row_scale(ins[1], x_ref.shape[1])
    if scaled_tiles:
        acc = acc * jnp.where(pl.program_id(1) < scaled_tiles, factor, 1.0)
    o_ref[...] = acc.astype(o_ref.dtype)
    _run_cast_jobs(job_src, job_dst)


def matmul(x, ssq, w, *, scaled_cols=0, factor=1.0, jobs=(), bm=1024):
    m, k = x.shape
    n = w.shape[-1]
    bm = _pick(m, bm)
    bn = _pick_bn(n, k, bm, 1, 2, lambda bn: _cast_job_bytes(jobs, (m // bm) * (n // bn)))
    assert scaled_cols % bn == 0
    grid = (m // bm, n // bn)
    normed = ssq is not None
    operands = (x, ssq, w) if normed else (x, w)
    in_specs = [pl.BlockSpec((bm, k), lambda i, j: (i, 0))]
    if normed:
        in_specs.append(pl.BlockSpec((bm, SSQ_LANES), lambda i, j: (i, 0)))
    in_specs.append(_weight_spec(k, bn))
    job_in, job_out, job_shape = _cast_job_specs(jobs, grid)
    outs = pl.pallas_call(
        functools.partial(_mm_kernel, scaled_tiles=scaled_cols // bn, factor=factor,
                          normed=normed, n_jobs=len(jobs)),
        grid=grid,
        in_specs=in_specs + job_in,
        out_specs=[pl.BlockSpec((bm, bn), lambda i, j: (i, j))] + job_out,
        out_shape=[jax.ShapeDtypeStruct((m, n), BF16)] + job_shape,
        compiler_params=_compiler_params(("arbitrary", "arbitrary")),
        name="matmul",
    )(*operands, *[w for w, _ in jobs])
    return outs[0], list(outs[1:])


def _mm_swiglu_kernel(*refs, normed, n_jobs):
    ins, job_src, (o_ref,), job_dst = _split_refs(refs, 4 if normed else 3, 1, n_jobs)
    x_ref, wg_ref, wu_ref = ins[0], ins[-2], ins[-1]
    x = x_ref[...]
    gate = jnp.dot(x, wg_ref[...], preferred_element_type=F32)
    up = jnp.dot(x, wu_ref[...], preferred_element_type=F32)
    if normed:
        r = _row_scale(ins[1], x_ref.shape[1])
        gate = gate * r
        up = up * r
    o_ref[...] = (gate * jax.nn.sigmoid(gate) * up).astype(o_ref.dtype)
    _run_cast_jobs(job_src, job_dst)


def matmul_swiglu(x, ssq, w_in, *, jobs=(), bm=1024):
    m, k = x.shape
    f = w_in.shape[-1] // 2
    bm = _pick(m, bm)
    bn = _pick_bn(f, k, bm, 2, 2, lambda bn: _cast_job_bytes(jobs, (m // bm) * (f // bn)))
    nf = f // bn
    grid = (m // bm, nf)
    normed = ssq is not None
    operands = (x, ssq, w_in, w_in) if normed else (x, w_in, w_in)
    in_specs = [pl.BlockSpec((bm, k), lambda i, j: (i, 0))]
    if normed:
        in_specs.append(pl.BlockSpec((bm, SSQ_LANES), lambda i, j: (i, 0)))
    in_specs += [_weight_spec(k, bn), _weight_spec(k, bn, nf)]
    job_in, job_out, job_shape = _cast_job_specs(jobs, grid)
    outs = pl.pallas_call(
        functools.partial(_mm_swiglu_kernel, normed=normed, n_jobs=len(jobs)),
        grid=grid,
        in_specs=in_specs + job_in,
        out_specs=[pl.BlockSpec((bm, bn), lambda i, j: (i, j))] + job_out,
        out_shape=[jax.ShapeDtypeStruct((m, f), BF16)] + job_shape,
        compiler_params=_compiler_params(("arbitrary", "arbitrary")),
        name="matmul_swiglu",
    )(*operands, *[w for w, _ in jobs])
    return outs[0], list(outs[1:])


def _mm_residual_kernel(*refs, alpha, n_gains, n_jobs):
    ins, job_src, outs, job_dst = _split_refs(refs, 3 + n_gains, 1 + n_gains + bool(n_gains),
                                              n_jobs)
    x_ref, w_ref, h_ref = ins[:3]
    g_refs = ins[3:]
    o_ref = outs[0]
    og_refs = outs[1:1 + n_gains]
    if n_gains:
        ssq_ref = outs[-1]

        @pl.when(pl.program_id(1) == 0)
        def _():
            ssq_ref[...] = jnp.zeros_like(ssq_ref)

    acc = jnp.dot(x_ref[...], w_ref[...], preferred_element_type=F32)
    out = h_ref[...] + alpha * acc
    o_ref[...] = out
    _run_cast_jobs(job_src, job_dst)
    if n_gains:
        for g_ref, og_ref in zip(g_refs, og_refs):
            og_ref[...] = (out * g_ref[...]).astype(og_ref.dtype)
        sq = out * out
        part = sq[:, :SSQ_LANES]
        for c in range(1, out.shape[1] // SSQ_LANES):
            part = part + sq[:, c * SSQ_LANES:(c + 1) * SSQ_LANES]
        ssq_ref[...] += part


def matmul_residual(x, w, h, alpha, gains=(), *, jobs=(), bm=1024):
    m, k = x.shape
    n = w.shape[-1]
    bm = _pick(m, bm)
    bn = _pick_bn(n, k, bm, 1, 4 + 4 + 2 * len(gains),
                  lambda bn: _cast_job_bytes(jobs, (m // bm) * (n // bn)))
    grid = (m // bm, n // bn)
    tile = pl.BlockSpec((bm, bn), lambda i, j: (i, j))
    in_specs = [pl.BlockSpec((bm, k), lambda i, j: (i, 0)), _weight_spec(k, bn), tile]
    in_specs += [pl.BlockSpec((1, bn), lambda i, j: (0, j))] * len(gains)
    out_specs = [tile] * (1 + len(gains))
    out_shape = [jax.ShapeDtypeStruct((m, n), F32)]
    out_shape += [jax.ShapeDtypeStruct((m, n), BF16)] * len(gains)
    if gains:
        assert bn % SSQ_LANES == 0
        out_specs.append(pl.BlockSpec((bm, SSQ_LANES), lambda i, j: (i, 0)))
        out_shape.append(jax.ShapeDtypeStruct((m, SSQ_LANES), F32))
    n_main = len(out_specs)
    job_in, job_out, job_shape = _cast_job_specs(jobs, grid)
    outs = pl.pallas_call(
        functools.partial(_mm_residual_kernel, alpha=alpha, n_gains=len(gains),
                          n_jobs=len(jobs)),
        grid=grid,
        in_specs=in_specs + job_in,
        out_specs=out_specs + job_out,
        out_shape=out_shape + job_shape,
        compiler_params=_compiler_params(("arbitrary", "arbitrary")),
        name="matmul_residual",
    )(x, w, h, *[g.reshape(1, n) for g in gains], *[w for w, _ in jobs])
    ssq = outs[n_main - 1] if gains else None
    return outs[0], list(outs[1:1 + len(gains)]), ssq, list(outs[n_main:])


def _sb_attention_kernel(q_ref, k_ref, v_ref, o_ref, kt_ref, acc_ref,
                         *, seq, blk, group):
    nblk = seq // blk
    for g in range(group):
        lanes = slice(g * HEAD_DIM, (g + 1) * HEAD_DIM)
        for b in range(nblk):
            rows = slice(b * blk, (b + 1) * blk)
            kt_ref[g, b] = k_ref[rows, lanes].T

    row = lax.broadcasted_iota(jnp.int32, (blk, blk), 0)
    col = lax.broadcasted_iota(jnp.int32, (blk, blk), 1)
    lower = (row > col).astype(BF16)
    strict = col < row

    def blocks(i, j, nkb, carries, masked):
        chains = [(g, b) for g in range(group) for b in range(nkb)]
        lanes = [slice(g * HEAD_DIM, (g + 1) * HEAD_DIM) for g in range(group)]
        q_rows = pl.ds(pl.multiple_of(i * blk, blk), blk)
        zs = [jnp.dot(q_ref[q_rows, lanes[g]], kt_ref[g, j - b],
                      preferred_element_type=F32) for g, b in chains]
        log_betas, splits, before = [], [], []
        carries = list(carries)
        for c, (g, b) in enumerate(chains):
            z = zs[c]
            neg_abs = lax.bitcast_convert_type(
                lax.bitcast_convert_type(z, jnp.uint32) | jnp.uint32(0x80000000), F32)
            log_beta = jnp.minimum(z, 0.0) - jnp.log(1.0 + jnp.exp2(neg_abs)) * LOG2_E
            log_1m = log_beta - z
            if masked and b == 0:
                log_1m = jnp.where(strict, log_1m, 0.0)
            log_betas.append(log_beta)
            splits.append(log_1m.astype(BF16))
            before.append(carries[g])
            carries[g] = carries[g] + jnp.sum(log_1m, axis=1, keepdims=True)
        laters = [jnp.dot(split, lower, preferred_element_type=F32) for split in splits]
        weights = []
        for c, (g, b) in enumerate(chains):
            a = jnp.exp2(log_betas[c] + (laters[c] + before[c]))
            if masked and b == 0:
                a = jnp.where(strict, a, 0.0)
            weights.append(a.astype(BF16))
        for g in range(group):
            pv = None
            for c, (cg, b) in enumerate(chains):
                if cg == g:
                    k_rows = pl.ds(pl.multiple_of((j - b) * blk, blk), blk)
                    term = jnp.dot(weights[c], v_ref[k_rows, lanes[g]],
                                   preferred_element_type=F32)
                    pv = term if pv is None else pv + term
            if masked:
                acc_ref[g] = pv
            else:
                acc_ref[g] += pv
        return tuple(carries)

    def any_live(carries):
        top = carries[0]
        for c in carries[1:]:
            top = jnp.maximum(top, c)
        return jnp.max(top) > EXP2_UNDERFLOW

    def write_out(i):
        out_rows = pl.ds(pl.multiple_of(i * blk, blk), blk)
        for g in range(group):
            o_ref[out_rows, g * HEAD_DIM:(g + 1) * HEAD_DIM] = acc_ref[g].astype(o_ref.dtype)

    zero = jnp.zeros((blk, 1), F32)
    blocks(0, 0, 1, (zero,) * group, True)
    write_out(0)

    def q_block(i, _):
        carries = blocks(i, i, 2, (zero,) * group, True)

        def cond(state):
            return jnp.logical_and(state[0] < i, state[1])

        def body(state):
            n = state[0]
            new = blocks(i, i - 1 - n, 1, state[2:], False)
            return (n + 1, any_live(new)) + new

        lax.while_loop(cond, body, (jnp.int32(1), any_live(carries)) + carries)
        write_out(i)
        return 0

    lax.fori_loop(1, nblk, q_block, 0)


def sb_attention(qkv, batch, seq):
    d = qkv.shape[1] // 3
    heads = d // HEAD_DIM
    group = math.gcd(heads, SB_HEADS_PER_STEP)
    steps = heads // group
    width = group * HEAD_DIM
    blk = _pick(seq, ATTN_BLOCK)
    nblk = seq // blk
    kernel = functools.partial(_sb_attention_kernel, seq=seq, blk=blk, group=group)
    return pl.pallas_call(
        kernel,
        grid=(batch, steps),
        in_specs=[pl.BlockSpec((seq, width), lambda b, h: (b, h)),
                  pl.BlockSpec((seq, width), lambda b, h: (b, steps + h)),
                  pl.BlockSpec((seq, width), lambda b, h: (b, 2 * steps + h))],
        out_specs=pl.BlockSpec((seq, width), lambda b, h: (b, h)),
        out_shape=jax.ShapeDtypeStruct((batch * seq, d), BF16),
        scratch_shapes=[pltpu.VMEM((group, nblk, HEAD_DIM, blk), BF16),
                        pltpu.VMEM((group, blk, HEAD_DIM), F32)],
        compiler_params=_compiler_params(("arbitrary", "arbitrary")),
        name="sb_attention",
    )(qkv, qkv, qkv)


def _diff_attention_kernel(lq1_ref, lk1_ref, lq2_ref, lk2_ref, g_ref, q_ref, k_ref, v_ref,
                           o_ref, qt_ref, vt_ref, acc_ref, bias_ref,
                           *, seq, blk, group, heads, lam_init):
    nblk = seq // blk
    dv = 2 * HEAD_DIM
    for g in range(group):
        for b in range(nblk):
            rows = slice(b * blk, (b + 1) * blk)
            for c in range(2):
                lanes = slice(g * dv + c * HEAD_DIM, g * dv + (c + 1) * HEAD_DIM)
                qt_ref[2 * g + c, b] = q_ref[rows, lanes].T
            vt_ref[g, b] = v_ref[rows, g * dv:(g + 1) * dv].T

    s1 = jnp.sum(lq1_ref[...] * lk1_ref[...], axis=-1, keepdims=True)
    s2 = jnp.sum(lq2_ref[...] * lk2_ref[...], axis=-1, keepdims=True)
    lam = jnp.exp(s1) - jnp.exp(s2) + lam_init
    out_gain = g_ref[...] * (1.0 - lam_init)

    first_head = jnp.full((1, 1), pl.program_id(1) * group, jnp.int32).astype(F32)
    slopes = [jnp.exp2(-8.0 * (first_head + (g + 1.0)) / heads) * LOG2_E
              for g in range(group)]

    row = lax.broadcasted_iota(jnp.int32, (blk, blk), 0)
    col = lax.broadcasted_iota(jnp.int32, (blk, blk), 1)
    dist = (col - row).astype(F32)
    allowed = (row // STREAM_CHUNK) <= (col // STREAM_CHUNK)
    for g in range(group):
        bias_ref[2 * g] = slopes[g] * dist
        bias_ref[2 * g + 1] = jnp.where(allowed, slopes[g] * jnp.abs(dist), jnp.inf)

    nset = 2 * group

    def update(i, j, nkb, slot, state, masked):
        sets = range(nset)
        kbs = range(nkb)
        zs = []
        for u in sets:
            g, c = divmod(u, 2)
            lanes = slice(g * dv + c * HEAD_DIM, g * dv + (c + 1) * HEAD_DIM)
            for b in kbs:
                k_rows = pl.ds(pl.multiple_of((j + b) * blk, blk), blk)
                zs.append(jnp.dot(k_ref[k_rows, lanes], qt_ref[u, i],
                                  preferred_element_type=F32))
        new_state, alphas, ps = (), [], []
        for u in sets:
            m, l = state[2 * u], state[2 * u + 1]
            if masked:
                ys = [zs[u] - bias_ref[2 * (u // 2) + 1]]
                fars = [0.0]
            else:
                fars = [slopes[u // 2] * jnp.asarray((i - j - b) * blk, F32) for b in kbs]
                ys = [zs[u * nkb + b] - bias_ref[2 * (u // 2)] for b in kbs]
            m_new = m
            for y, far in zip(ys, fars):
                m_new = jnp.maximum(m_new, jnp.max(y, axis=0, keepdims=True) - far)
            alpha = jnp.exp2(m - m_new)
            l = l * alpha
            for y, far in zip(ys, fars):
                p = jnp.exp2(y - (m_new + far))
                l = l + jnp.sum(p, axis=0, keepdims=True)
                ps.append(p.astype(BF16))
            new_state += (m_new, l)
            alphas.append(alpha)
        for u in sets:
            pv = jnp.dot(vt_ref[u // 2, j], ps[u * nkb], preferred_element_type=F32)
            for b in range(1, nkb):
                pv += jnp.dot(vt_ref[u // 2, j + b], ps[u * nkb + b],
                              preferred_element_type=F32)
            if masked:
                acc_ref[slot, u] = pv
            else:
                acc_ref[slot, u] = acc_ref[slot, u] * alphas[u] + pv
        return new_state

    def finalize(i, slot, state):
        out_rows = pl.ds(pl.multiple_of(i * blk, blk), blk)
        for g in range(group):
            inv1, inv2 = 1.0 / state[4 * g + 1], lam / state[4 * g + 3]
            ot = acc_ref[slot, 2 * g] * inv1 - acc_ref[slot, 2 * g + 1] * inv2
            o = ot.T
            ms = jnp.mean(o * o, axis=-1, keepdims=True)
            y = o * lax.rsqrt(ms + RMS_EPS) * out_gain
            o_ref[out_rows, g * dv:(g + 1) * dv] = y.astype(o_ref.dtype)

    neg = jnp.full((1, blk), -jnp.inf, F32)
    zero = jnp.zeros((1, blk), F32)
    fresh = (neg, zero) * nset

    def q_block(i, prev):
        slot = i % 2
        finalize(i - 1, 1 - slot, prev)
        state = update(i, i, 1, slot, fresh, True)

        def pair(n, st):
            return update(i, i - 2 - 2 * n, 2, slot, st, False)

        def single(n, st):
            return update(i, 0, 1, slot, st, False)

        state = lax.fori_loop(0, i // 2, pair, state)
        return lax.fori_loop(0, i % 2, single, state)

    first = update(0, 0, 1, 0, fresh, True)
    last = lax.fori_loop(1, nblk, q_block, first)
    finalize(nblk - 1, (nblk - 1) % 2, last)


def diff_attention(q, kv, lq1, lk1, lq2, lk2, gain, batch, seq, lam_init):
    d = q.shape[1]
    dv = 2 * HEAD_DIM
    heads = d // dv
    group = math.gcd(heads, DIFF_HEADS_PER_STEP)
    steps = heads // group
    width = group * dv
    blk = _pick(seq, ATTN_BLOCK)
    nblk = seq // blk
    kernel = functools.partial(_diff_attention_kernel, seq=seq, blk=blk, group=group,
                               heads=heads, lam_init=lam_init)
    vec = pl.BlockSpec((1, HEAD_DIM), lambda b, h: (0, 0))
    head_blk = pl.BlockSpec((seq, width), lambda b, h: (b, h))
    return pl.pallas_call(
        kernel,
        grid=(batch, steps),
        in_specs=[vec, vec, vec, vec,
                  pl.BlockSpec((1, dv), lambda b, h: (0, 0)),
                  head_blk, head_blk,
                  pl.BlockSpec((seq, width), lambda b, h: (b, steps + h))],
        out_specs=head_blk,
        out_shape=jax.ShapeDtypeStruct((batch * seq, d), BF16),
        scratch_shapes=[pltpu.VMEM((2 * group, nblk, HEAD_DIM, blk), BF16),
                        pltpu.VMEM((group, nblk, dv, blk), BF16),
                        pltpu.VMEM((2, 2 * group, dv, blk), F32),
                        pltpu.VMEM((2 * group, blk, blk), F32)],
        compiler_params=_compiler_params(("arbitrary", "arbitrary")),
        name="diff_attention",
    )(lq1.reshape(1, -1), lk1.reshape(1, -1), lq2.reshape(1, -1), lk2.reshape(1, -1),
      gain.reshape(1, -1), q, kv, kv)


def kernel(x, ffn_norm, w_ffn_in, w_ffn_out, attn_norm, w_qkv_a, w_o_a, kv_norm, w_kv_b, w_q_b,
           lambda_q1, lambda_k1, lambda_q2, lambda_k2, subln_norm, w_o_b, final_norm):
    batch, seq, d = x.shape
    depth = ffn_norm.shape[0]
    n_a = w_qkv_a.shape[0]
    h = x.reshape(batch * seq, d)
    j_of = lambda l: l - n_a

    ready = {}

    def job(name, *index):
        src = {"in": w_ffn_in, "out": w_ffn_out, "qkv": w_qkv_a, "oa": w_o_a, "kv": w_kv_b,
               "qb": w_q_b, "ob": w_o_b}[name]
        return (name,) + index, (src, index)

    def run(call, keyed_jobs):
        *main, cast = call(jobs=tuple(j for _, j in keyed_jobs))
        ready.update({key: w for (key, _), w in zip(keyed_jobs, cast)})
        return main

    ready[("in", 0, 0)] = w_ffn_in[0, 0].astype(BF16)
    xg, ssq = rmsnorm(h, ffn_norm[0, 0], BF16), None
    kv_shared = None
    for l in range(depth):
        sb_layer = l < n_a
        proj = job("qkv", l) if sb_layer else job("qb", j_of(l))
        out_proj = job("oa", l) if sb_layer else job("ob", j_of(l))
        (hid,) = run(functools.partial(matmul_swiglu, xg, ssq, ready[("in", l, 0)]),
                     [job("out", l, 0), proj])
        h, (xg,), ssq = matmul_residual(hid, ready[("out", l, 0)], h, 0.5, (attn_norm[l],))[:3]
        (qx,) = run(functools.partial(matmul, xg, ssq, ready[proj[0]], scaled_cols=d,
                                      factor=QK_LOG2_SCALE), [out_proj])
        if sb_layer:
            o = sb_attention(qx, batch, seq)
        else:
            lam_init = 0.8 - 0.6 * math.exp(-0.3 * l)
            jj = j_of(l)
            o = diff_attention(qx, kv_shared, lambda_q1[jj], lambda_k1[jj], lambda_q2[jj],
                               lambda_k2[jj], subln_norm[jj], batch, seq, lam_init)
        h, (xg,), ssq = run(functools.partial(matmul_residual, o, ready[out_proj[0]], h, 1.0,
                                              (ffn_norm[l, 1],)), [job("in", l, 1)])
        later = [job("out", l, 1)]
        if l + 1 < depth:
            later.append(job("in", l + 1, 0))
        if l == n_a - 1:
            later.append(job("kv"))
        (hid,) = run(functools.partial(matmul_swiglu, xg, ssq, ready[("in", l, 1)]), later)
        gains = (ffn_norm[l + 1, 0],) if l + 1 < depth else ()
        if l == n_a - 1:
            gains += (kv_norm,)
        h, normed, ssq = matmul_residual(hid, ready[("out", l, 1)], h, 0.5, gains)[:3]
        xg = normed[0] if l + 1 < depth else None
        if l == n_a - 1:
            kv_shared = matmul(normed[-1], ssq, ready[("kv",)])[0]
    return rmsnorm(h, final_norm, F32).reshape(batch, seq, d)
```
